```python
import math
import jax, jax.numpy as jnp
from jax import lax
import numpy as np

D_MODEL = 2048
BATCH = 4
SEQ = 2048
DEPTH = 4
DEC_BATCH = 128
DEC_SEQ = 4
PAST_LEN = 16384
PAGE_SIZE = 128

MIX_WIDTH = D_MODEL
POOL_WIDTH = MIX_WIDTH // 4
POOL_WINDOWS = (2, 4, 8, 16)
POOL_GROUPS = len(POOL_WINDOWS)
POOL_GROUP_DIM = POOL_WIDTH // POOL_GROUPS
POOL_BUF = max(POOL_WINDOWS) - 1
SSD_WIDTH = MIX_WIDTH - POOL_WIDTH
SSD_HEAD_DIM = 64
SSD_HEADS = SSD_WIDTH // SSD_HEAD_DIM
SSD_GROUPS = 4
SSD_HPG = SSD_HEADS // SSD_GROUPS
D_STATE = 128
SSD_CONV = 4
SSD_CHUNK = 128
XBC_WIDTH = SSD_WIDTH + 2 * SSD_GROUPS * D_STATE
IN_WIDTH = POOL_WIDTH + SSD_WIDTH + XBC_WIDTH + SSD_HEADS
D_FF = 128 * ((8 * D_MODEL // 3 + 127) // 128)
FFN_CONV = 3
EPS = 1e-6

kernel_name = 'hybrid_pool_ssd_convffn_adaln_step'


def rmsnorm(x, g):
    xf = x.astype(jnp.float32)
    y = xf * lax.rsqrt(jnp.mean(xf * xf, axis=-1, keepdims=True) + EPS)
    return (y * g.astype(jnp.float32)).astype(x.dtype)


def causal_dwconv(u, prev, w, b):
    k = w.shape[0]
    L = u.shape[1]
    full = jnp.concatenate([prev.astype(u.dtype), u], axis=1)
    out = b.astype(u.dtype)
    for i in range(k):
        out = out + full[:, i:i + L] * w[i].astype(u.dtype)
    return out, full[:, full.shape[1] - (k - 1):]


def pool_mixer(u, prev, pos0, pool_w, pool_scale):
    L = u.shape[1]
    full = jnp.concatenate([prev.astype(u.dtype), u], axis=1)
    cs = jnp.cumsum(full.astype(jnp.float32), axis=1)
    cs = jnp.pad(cs, ((0, 0), (1, 0), (0, 0)))
    pos = pos0 + jnp.arange(L, dtype=jnp.int32)
    outs = []
    for g, w in enumerate(POOL_WINDOWS):
        lo, hi = g * POOL_GROUP_DIM, (g + 1) * POOL_GROUP_DIM
        s = cs[:, POOL_BUF + 1:POOL_BUF + 1 + L, lo:hi] - cs[:, POOL_BUF + 1 - w:POOL_BUF + 1 - w + L, lo:hi]
        cnt = jnp.minimum(pos + 1, w).astype(jnp.float32)[None, :, None]
        d = s / cnt - u[:, :, lo:hi].astype(jnp.float32)
        outs.append(d @ pool_w[g].astype(jnp.float32))
    out = jnp.concatenate(outs, axis=-1) * pool_scale.astype(jnp.float32)
    return out.astype(u.dtype), full[:, full.shape[1] - POOL_BUF:]


def ssd_chunked(xdt, dA, Bm, Cm, h0):
    b, L = xdt.shape[0], xdt.shape[1]
    q = min(SSD_CHUNK, L)
    nc = -(-L // q)
    pad = nc * q - L

    def padc(a):
        a = jnp.pad(a, ((0, 0), (0, pad)) + ((0, 0),) * (a.ndim - 2))
        return a.reshape((b, nc, q) + a.shape[2:])

    xdt, dA, Bm, Cm = padc(xdt), padc(dA), padc(Bm), padc(Cm)
    cs = jnp.cumsum(dA, axis=2)
    seg = cs[:, :, :, None] - cs[:, :, None, :]
    mask = jnp.tril(jnp.ones((q, q), dtype=bool))[:, :, None, None]
    Lm = jnp.exp(jnp.where(mask, seg, -jnp.inf))
    CB = jnp.einsum('bclgn,bcsgn->bclsg', Cm, Bm)
    y_diag = jnp.einsum('bclsg,bclsgr,bcsgrp->bclgrp', CB, Lm, xdt)
    decay_to_end = jnp.exp(cs[:, :, -1:] - cs)
    chunk_states = jnp.einsum('bclgn,bclgr,bclgrp->bcgrpn', Bm, decay_to_end, xdt)
    chunk_decay = jnp.exp(cs[:, :, -1])

    def step(h, inp):
        s, dec = inp
        return h * dec[..., None, None] + s, h

    h_final, h_starts = lax.scan(step, h0, (jnp.moveaxis(chunk_states, 1, 0), jnp.moveaxis(chunk_decay, 1, 0)))
    h_starts = jnp.moveaxis(h_starts, 0, 1)
    y_off = jnp.einsum('bclgn,bcgrpn,bclgr->bclgrp', Cm, h_starts, jnp.exp(cs))
    y = (y_diag + y_off).reshape((b, nc * q) + xdt.shape[3:])[:, :L]
    return y, h_final


def ssd_mixer(z, xbc, dt_raw, conv_prev, h0, conv_w, conv_b, dt_bias, a_log, d_skip, norm_g):
    b, L = z.shape[0], z.shape[1]
    xbc_c, new_conv = causal_dwconv(xbc, conv_prev, conv_w, conv_b)
    xbc_c = jax.nn.silu(xbc_c.astype(jnp.float32))
    xs = xbc_c[..., :SSD_WIDTH].reshape(b, L, SSD_GROUPS, SSD_HPG, SSD_HEAD_DIM)
    Bm = xbc_c[..., SSD_WIDTH:SSD_WIDTH + SSD_GROUPS * D_STATE].reshape(b, L, SSD_GROUPS, D_STATE)
    Cm = xbc_c[..., SSD_WIDTH + SSD_GROUPS * D_STATE:].reshape(b, L, SSD_GROUPS, D_STATE)
    dt = jax.nn.softplus(dt_raw.astype(jnp.float32) + dt_bias.astype(jnp.float32)).reshape(b, L, SSD_GROUPS, SSD_HPG)
    A = -jnp.exp(a_log.astype(jnp.float32)).reshape(SSD_GROUPS, SSD_HPG)
    h0 = h0.astype(jnp.float32).reshape(b, SSD_GROUPS, SSD_HPG, SSD_HEAD_DIM, D_STATE)
    y, h_final = ssd_chunked(xs * dt[..., None], dt * A, Bm, Cm, h0)
    y = y + d_skip.astype(jnp.float32).reshape(SSD_GROUPS, SSD_HPG)[..., None] * xs
    y = y.reshape(b, L, SSD_WIDTH) * jax.nn.silu(z.astype(jnp.float32))
    y = rmsnorm(y, norm_g)
    return y.astype(z.dtype), new_conv, h_final.reshape(b, SSD_HEADS, SSD_HEAD_DIM, D_STATE)


def conv_ffn(h, prev, w_up, conv_w, conv_b, w_down):
    up = h @ w_up
    gate, val = up[..., :D_FF], up[..., D_FF:]
    gate_c, new_buf = causal_dwconv(gate, prev, conv_w, conv_b)
    act = jax.nn.silu(gate_c) * val
    return act @ w_down, new_buf


def block(x, c, st_pool, st_conv, st_ssm, st_ffn, pos0, lp):
    mod = (jax.nn.silu(c.astype(jnp.float32)) @ lp['w_ada'].astype(jnp.float32) + lp['b_ada'].astype(jnp.float32)).astype(x.dtype)
    sh1, sc1, g1, sh2, sc2, g2 = [m[:, None] for m in jnp.split(mod, 6, axis=-1)]
    h = rmsnorm(x, lp['norm1']) * (1 + sc1) + sh1
    proj = h @ lp['w_in']
    u_pool, z, xbc, dt_raw = jnp.split(proj, [POOL_WIDTH, POOL_WIDTH + SSD_WIDTH, POOL_WIDTH + SSD_WIDTH + XBC_WIDTH], axis=-1)
    pool_out, new_pool = pool_mixer(u_pool, st_pool, pos0, lp['pool_w'], lp['pool_scale'])
    ssd_out, new_conv, new_ssm = ssd_mixer(z, xbc, dt_raw, st_conv, st_ssm, lp['conv_w'], lp['conv_b'],
                                           lp['dt_bias'], lp['a_log'], lp['d_skip'], lp['ssd_norm'])
    mix = jnp.concatenate([pool_out, ssd_out], axis=-1) @ lp['w_out']
    x = x + g1 * mix
    h2 = rmsnorm(x, lp['norm2']) * (1 + sc2) + sh2
    ffn_out, new_ffn = conv_ffn(h2, st_ffn, lp['w_up'], lp['ffn_conv_w'], lp['ffn_conv_b'], lp['w_down'])
    x = x + g2 * ffn_out
    return x, new_pool, new_conv, new_ssm, new_ffn


def setup_inputs(seed: int = 0) -> dict:
    key = jax.random.key(seed)
    ks = jax.random.split(key, 32)

    def nrm(k, shape, scale=1.0):
        return scale * jax.random.normal(k, shape, jnp.float32)

    dt0 = jnp.exp(jax.random.uniform(ks[14], (DEPTH, SSD_HEADS), jnp.float32, math.log(1e-3), math.log(1e-1)))
    return {
        'x_prompt': nrm(ks[0], (BATCH, SEQ, D_MODEL)),
        'x_sample': nrm(ks[1], (DEC_BATCH, DEC_SEQ, D_MODEL)),
        'c_prompt': nrm(ks[2], (BATCH, D_MODEL)),
        'c_sample': nrm(ks[3], (DEC_BATCH, D_MODEL)),
        'state_pool': nrm(ks[4], (DEPTH, DEC_BATCH, POOL_BUF, POOL_WIDTH)),
        'state_conv': nrm(ks[5], (DEPTH, DEC_BATCH, SSD_CONV - 1, XBC_WIDTH)),
        'state_ssm': nrm(ks[6], (DEPTH, DEC_BATCH, SSD_HEADS, SSD_HEAD_DIM, D_STATE), 0.1),
        'state_ffn': nrm(ks[7], (DEPTH, DEC_BATCH, FFN_CONV - 1, D_FF)),
        'norm1': 1.0 + nrm(ks[8], (DEPTH, D_MODEL), 0.02),
        'norm2': 1.0 + nrm(ks[9], (DEPTH, D_MODEL), 0.02),
        'w_ada': nrm(ks[10], (DEPTH, D_MODEL, 6 * D_MODEL), 0.5 * D_MODEL ** -0.5),
        'b_ada': nrm(ks[11], (DEPTH, 6 * D_MODEL), 0.02),
        'w_in': nrm(ks[12], (DEPTH, D_MODEL, IN_WIDTH), D_MODEL ** -0.5),
        'pool_w': nrm(ks[13], (DEPTH, POOL_GROUPS, POOL_GROUP_DIM, POOL_GROUP_DIM), POOL_GROUP_DIM ** -0.5),
        'pool_scale': 1.0 + nrm(ks[15], (DEPTH, POOL_WIDTH), 0.1),
        'conv_w': nrm(ks[16], (DEPTH, SSD_CONV, XBC_WIDTH), SSD_CONV ** -0.5),
        'conv_b': nrm(ks[17], (DEPTH, XBC_WIDTH), 0.02),
        'dt_bias': dt0 + jnp.log(-jnp.expm1(-dt0)),
        'a_log': jnp.log(jax.random.uniform(ks[18], (DEPTH, SSD_HEADS), jnp.float32, 1.0, 16.0)),
        'd_skip': 1.0 + nrm(ks[19], (DEPTH, SSD_HEADS), 0.1),
        'ssd_norm': 1.0 + nrm(ks[20], (DEPTH, SSD_WIDTH), 0.02),
        'w_out': nrm(ks[21], (DEPTH, MIX_WIDTH, D_MODEL), MIX_WIDTH ** -0.5),
        'w_up': nrm(ks[22], (DEPTH, D_MODEL, 2 * D_FF), D_MODEL ** -0.5),
        'ffn_conv_w': nrm(ks[23], (DEPTH, FFN_CONV, D_FF), FFN_CONV ** -0.5),
        'ffn_conv_b': nrm(ks[24], (DEPTH, D_FF), 0.02),
        'w_down': nrm(ks[25], (DEPTH, D_FF, D_MODEL), D_FF ** -0.5),
        'norm_f': 1.0 + nrm(ks[26], (D_MODEL,), 0.02),
    }


def reference(x_prompt, x_sample, c_prompt, c_sample, state_pool, state_conv, state_ssm, state_ffn,
              norm1, norm2, w_ada, b_ada, w_in, pool_w, pool_scale, conv_w, conv_b, dt_bias, a_log,
              d_skip, ssd_norm, w_out, w_up, ffn_conv_w, ffn_conv_b, w_down, norm_f):
    bp = x_prompt.shape[0]
    xp, xs = x_prompt, x_sample
    pp_pool, pp_conv, pp_ssm, pp_ffn = [], [], [], []
    ps_pool, ps_conv, ps_ssm, ps_ffn = [], [], [], []
    for l in range(DEPTH):
        lp = {'norm1': norm1[l], 'norm2': norm2[l], 'w_ada': w_ada[l], 'b_ada': b_ada[l], 'w_in': w_in[l],
              'pool_w': pool_w[l], 'pool_scale': pool_scale[l], 'conv_w': conv_w[l], 'conv_b': conv_b[l],
              'dt_bias': dt_bias[l], 'a_log': a_log[l], 'd_skip': d_skip[l], 'ssd_norm': ssd_norm[l],
              'w_out': w_out[l], 'w_up': w_up[l], 'ffn_conv_w': ffn_conv_w[l], 'ffn_conv_b': ffn_conv_b[l],
              'w_down': w_down[l]}
        xp, a, b_, c_, d_ = block(
            xp, c_prompt,
            jnp.zeros((bp, POOL_BUF, POOL_WIDTH), xp.dtype),
            jnp.zeros((bp, SSD_CONV - 1, XBC_WIDTH), xp.dtype),
            jnp.zeros((bp, SSD_HEADS, SSD_HEAD_DIM, D_STATE), jnp.float32),
            jnp.zeros((bp, FFN_CONV - 1, D_FF), xp.dtype),
            0, lp)
        pp_pool.append(a); pp_conv.append(b_); pp_ssm.append(c_); pp_ffn.append(d_)
        xs, a, b_, c_, d_ = block(xs, c_sample, state_pool[l], state_conv[l], state_ssm[l], state_ffn[l], PAST_LEN, lp)
        ps_pool.append(a); ps_conv.append(b_); ps_ssm.append(c_); ps_ffn.append(d_)
    y_prompt = rmsnorm(xp, norm_f)
    y_sample = rmsnorm(xs, norm_f)
    return (y_prompt, y_sample,
            jnp.stack(pp_pool), jnp.stack(pp_conv), jnp.stack(pp_ssm), jnp.stack(pp_ffn),
            jnp.stack(ps_pool), jnp.stack(ps_conv), jnp.stack(ps_ssm), jnp.stack(ps_ffn))
```

```python
import functools

import jax
import jax.numpy as jnp
from jax import lax
from jax.experimental import pallas as pl
from jax.experimental.pallas import tpu as pltpu

F32 = jnp.float32
BF16 = jnp.bfloat16

D_MODEL = 2048
BATCH = 4
SEQ = 2048
DEPTH = 4
DEC_BATCH = 128
DEC_SEQ = 4
PAST_LEN = 16384

POOL_WIDTH = 512
POOL_WINDOWS = (2, 4, 8, 16)
POOL_GROUP_DIM = 128
POOL_BUF = 15
SSD_WIDTH = 1536
SSD_HEAD_DIM = 64
SSD_HEADS = 24
SSD_GROUPS = 4
SSD_HPG = 6
D_STATE = 128
SSD_CONV = 4
XBC_WIDTH = 2560
MAIN_WIDTH = POOL_WIDTH + SSD_WIDTH + XBC_WIDTH
D_FF = 5504
FFN_CONV = 3
EPS = 1e-6

TP = BATCH * SEQ
TS = DEC_SEQ * DEC_BATCH
T = TP + TS
GROUP_COLS = SSD_HPG * SSD_HEAD_DIM
LANES = 128
CHUNK = 128

VMEM_LIMIT = 56 * 1024 * 1024


def _silu(x):
    return x * jax.nn.sigmoid(x)


def _softplus(x):
    return jnp.maximum(x, 0.0) + jnp.log1p(jnp.exp(-jnp.abs(x)))


def _cparams(sem):
    return pltpu.CompilerParams(dimension_semantics=sem, vmem_limit_bytes=VMEM_LIMIT)


def _rowmod(p_ref, s_ref, is_sample, tm):
    s = s_ref[...]
    if tm > DEC_BATCH:
        s = jnp.concatenate([s] * (tm // DEC_BATCH), axis=0)
    return jnp.where(is_sample, s, p_ref[...])


def _rms_mod(x, gain, sc, sh):
    ms = jnp.mean(x * x, axis=-1, keepdims=True)
    y = x * lax.rsqrt(ms + EPS) * gain
    return y * (1.0 + sc) + sh


def _ada_kernel(c_ref, w_ref, b_ref, os_ref, op_ref):
    a = _silu(c_ref[...]).astype(BF16)
    m = jnp.dot(a, w_ref[...].astype(BF16), preferred_element_type=F32) + b_ref[...]
    os_ref[...] = m[:DEC_BATCH]
    op_ref[...] = m[DEC_BATCH:DEC_BATCH + BATCH]


def _ada(c_all, w_ada, b_ada):
    tn = 1024
    rows = c_all.shape[0]
    return pl.pallas_call(
        _ada_kernel,
        grid=(DEPTH, 6 * D_MODEL // tn),
        in_specs=[
            pl.BlockSpec((rows, D_MODEL), lambda l, j: (0, 0)),
            pl.BlockSpec((None, D_MODEL, tn), lambda l, j: (l, 0, j)),
            pl.BlockSpec((None, 1, tn), lambda l, j: (l, 0, j)),
        ],
        out_specs=[
            pl.BlockSpec((None, DEC_BATCH, tn), lambda l, j: (l, 0, j)),
            pl.BlockSpec((None, BATCH, tn), lambda l, j: (l, 0, j)),
        ],
        out_shape=[
            jax.ShapeDtypeStruct((DEPTH, DEC_BATCH, 6 * D_MODEL), F32),
            jax.ShapeDtypeStruct((DEPTH, BATCH, 6 * D_MODEL), F32),
        ],
        compiler_params=_cparams(("arbitrary", "arbitrary")),
        name="ada",
    )(c_all, w_ada, b_ada.reshape(DEPTH, 1, 6 * D_MODEL))


def _mod_specs(l, k, tm, row_of):
    tpb = SEQ // tm

    def pmap(*ids):
        return (l, jnp.minimum(row_of(*ids) // tpb, BATCH - 1), 0, k)

    def smap(*ids):
        return (l, 0, k)

    return (pl.BlockSpec((None, None, 1, D_MODEL), pmap), pl.BlockSpec((None, DEC_BATCH, D_MODEL), smap))


def _norm_kernel(x_ref, g_ref, scp, scs, shp, shs, o_ref, *, npt, tm):
    is_s = pl.program_id(0) >= npt
    sc = _rowmod(scp, scs, is_s, tm)
    sh = _rowmod(shp, shs, is_s, tm)
    o_ref[...] = _rms_mod(x_ref[...], g_ref[...], sc, sh).astype(o_ref.dtype)


def _norm0(x, gain, modp, mods, l):
    tm = 256
    npt = TP // tm
    row = lambda i: i
    scp, scs = _mod_specs(l, 1, tm, row)
    shp, shs = _mod_specs(l, 0, tm, row)
    return pl.pallas_call(
        functools.partial(_norm_kernel, npt=npt, tm=tm),
        grid=(T // tm,),
        in_specs=[pl.BlockSpec((tm, D_MODEL), lambda i: (i, 0)),
                  pl.BlockSpec((1, D_MODEL), lambda i: (0, 0)),
                  scp, scs, shp, shs],
        out_specs=pl.BlockSpec((tm, D_MODEL), lambda i: (i, 0)),
        out_shape=jax.ShapeDtypeStruct((T, D_MODEL), BF16),
        compiler_params=_cparams(("arbitrary",)),
        name="norm0",
    )(x, gain.reshape(1, D_MODEL), modp, mods, modp, mods)


def _inproj_kernel(h_ref, w_ref, wdt_ref, o_ref, dt_ref):
    h = h_ref[...]
    o_ref[...] = jnp.dot(h, w_ref[...], preferred_element_type=F32)

    @pl.when(pl.program_id(1) == 0)
    def _():
        dt_ref[...] = jnp.dot(h, wdt_ref[...], preferred_element_type=F32)


def _inproj(h, w_main, w_dt):
    tm, tn = 1024, 512
    return pl.pallas_call(
        _inproj_kernel,
        grid=(pl.cdiv(T, tm), MAIN_WIDTH // tn),
        in_specs=[pl.BlockSpec((tm, D_MODEL), lambda i, j: (i, 0)),
                  pl.BlockSpec((D_MODEL, tn), lambda i, j: (0, j)),
                  pl.BlockSpec((D_MODEL, LANES), lambda i, j: (0, 0))],
        out_specs=[pl.BlockSpec((tm, tn), lambda i, j: (i, j)),
                   pl.BlockSpec((tm, LANES), lambda i, j: (i, 0))],
        out_shape=[jax.ShapeDtypeStruct((T, MAIN_WIDTH), F32),
                   jax.ShapeDtypeStruct((T, LANES), F32)],
        compiler_params=_cparams(("arbitrary", "arbitrary")),
        name="inproj",
    )(h, w_main, w_dt)


def _expand_heads(v, rows):
    lo = lax.broadcasted_iota(jnp.int32, (rows, LANES), 1) < SSD_HEAD_DIM
    parts = []
    for j in range(SSD_HEADS // 2):
        a = jnp.broadcast_to(v[:, 2 * j:2 * j + 1], (rows, LANES))
        b = jnp.broadcast_to(v[:, 2 * j + 1:2 * j + 2], (rows, LANES))
        parts.append(jnp.where(lo, a, b))
    return jnp.concatenate(parts, axis=1)


def _split3(x):
    a = x.astype(BF16)
    r = x - a.astype(F32)
    b = r.astype(BF16)
    c = (r - b.astype(F32)).astype(BF16)
    return a, b, c


def _gated_norm(y, z, gain):
    y = y * _silu(z)
    ms = jnp.mean(y * y, axis=-1, keepdims=True)
    return y * lax.rsqrt(ms + EPS) * gain


def _mixp_kernel(proj_ref, dt_ref, poolw_ref, pscale_ref, cw_ref, cb_ref, dtb_ref, alog_ref, dskip_ref, ng_ref,
                 mix_ref, st_ref, xpad, upad, hT):
    c = pl.program_id(1)
    Q = CHUNK

    @pl.when(c == 0)
    def _():
        xpad[0:8, :] = jnp.zeros((8, XBC_WIDTH), F32)
        upad[0:16, :] = jnp.zeros((16, POOL_WIDTH), F32)
        hT[...] = jnp.zeros(hT.shape, F32)

    u = proj_ref[:, 0:POOL_WIDTH]
    upad[16:16 + Q, :] = u
    pos = c * Q + lax.broadcasted_iota(jnp.int32, (Q, 1), 0)
    outs = []
    for g, w in enumerate(POOL_WINDOWS):
        lo = g * POOL_GROUP_DIM
        ug = u[:, lo:lo + POOL_GROUP_DIM]
        s = ug
        for k in range(1, w):
            s = s + upad[16 - k:16 - k + Q, lo:lo + POOL_GROUP_DIM]
        cnt = jnp.minimum(pos + 1, w).astype(F32)
        d = s / cnt - ug
        outs.append(jnp.dot(d.astype(BF16), poolw_ref[g].astype(BF16), preferred_element_type=F32))
    mix_ref[:, 0:POOL_WIDTH] = (jnp.concatenate(outs, axis=1) * pscale_ref[...]).astype(mix_ref.dtype)
    upad[0:16, :] = upad[Q:Q + 16, :]

    xbc = proj_ref[:, POOL_WIDTH + SSD_WIDTH:MAIN_WIDTH]
    xpad[8:8 + Q, :] = xbc
    acc = cb_ref[...] + cw_ref[3:4, :] * xbc
    for i in range(SSD_CONV - 1):
        acc = acc + cw_ref[i:i + 1, :] * xpad[5 + i:5 + i + Q, :]
    xpad[0:8, :] = xpad[Q:Q + 8, :]
    xc = _silu(acc)
    xs = xc[:, 0:SSD_WIDTH]
    Bb = xc[:, SSD_WIDTH:SSD_WIDTH + SSD_GROUPS * D_STATE].astype(BF16)
    Cb = xc[:, SSD_WIDTH + SSD_GROUPS * D_STATE:XBC_WIDTH].astype(BF16)

    dt = _softplus(dt_ref[...] + dtb_ref[...])
    dA = dt * (-jnp.exp(alog_ref[...]))
    row = lax.broadcasted_iota(jnp.int32, (Q, Q), 0)
    col = lax.broadcasted_iota(jnp.int32, (Q, Q), 1)
    tri = row >= col
    tri_lo = jnp.where(tri, 1.0, 0.0).astype(BF16)
    tri_up = jnp.where(row <= col, 1.0, 0.0).astype(BF16)
    pieces = _split3(dA)
    cs = sum(jnp.dot(tri_lo, p, preferred_element_type=F32) for p in pieces)
    csT = sum(lax.dot_general(p, tri_up, (((0,), (0,)), ((), ())), preferred_element_type=F32)
              for p in pieces)
    cs_last = cs[Q - 1:Q, :]
    expcs_e = _expand_heads(jnp.exp(cs), Q)
    dte_e = _expand_heads(jnp.exp(cs_last - cs), Q)
    xdt = xs * _expand_heads(dt, Q)
    lane = lax.broadcasted_iota(jnp.int32, (1, SSD_WIDTH), 1)
    lo_mask = jnp.where((lane & (LANES - 1)) < SSD_HEAD_DIM, 1.0, 0.0)
    xdt_lo = (xdt * lo_mask).astype(BF16)
    xdt_hi = (xdt * (1.0 - lo_mask)).astype(BF16)
    xw_b = (xdt * dte_e).astype(BF16)
    decay_e = expcs_e[Q - 1:Q, :]
    h_prev = hT[...]
    h_prev_b = h_prev.astype(BF16)

    ys = []
    for g in range(SSD_GROUPS):
        gs = slice(g * GROUP_COLS, (g + 1) * GROUP_COLS)
        Cg = Cb[:, g * D_STATE:(g + 1) * D_STATE]
        Bg = Bb[:, g * D_STATE:(g + 1) * D_STATE]
        CB = lax.dot_general(Cg, Bg, (((1,), (1,)), ((), ())), preferred_element_type=F32)
        y_off = jnp.dot(Cg, h_prev_b[:, gs], preferred_element_type=F32)
        ST = lax.dot_general(Bg, xw_b[:, gs], (((0,), (0,)), ((), ())), preferred_element_type=F32)
        hT[:, gs] = h_prev[:, gs] * decay_e[:, gs] + ST
        parts = []
        for j in range(SSD_HPG // 2):
            h1 = SSD_HPG * g + 2 * j
            ps = slice(h1 * SSD_HEAD_DIM, h1 * SSD_HEAD_DIM + LANES)
            W = []
            for h in (h1, h1 + 1):
                seg = cs[:, h:h + 1] - csT[h:h + 1, :]
                W.append((CB * jnp.exp(jnp.where(tri, seg, -jnp.inf))).astype(BF16))
            parts.append(jnp.dot(W[0], xdt_lo[:, ps], preferred_element_type=F32)
                         + jnp.dot(W[1], xdt_hi[:, ps], preferred_element_type=F32))
        ys.append(jnp.concatenate(parts, axis=1) + y_off * expcs_e[:, gs])
    y = jnp.concatenate(ys, axis=1) + dskip_ref[...] * xs
    z = proj_ref[:, POOL_WIDTH:POOL_WIDTH + SSD_WIDTH]
    mix_ref[:, POOL_WIDTH:D_MODEL] = _gated_norm(y, z, ng_ref[...]).astype(mix_ref.dtype)

    @pl.when(c == pl.num_programs(1) - 1)
    def _():
        st_ref[...] = hT[...]


def _mix_prompt(proj, dtp, lw):
    nc = SEQ // CHUNK
    full = lambda shape: pl.BlockSpec(shape, lambda b, c: (0,) * len(shape))
    return pl.pallas_call(
        _mixp_kernel,
        grid=(BATCH, nc),
        in_specs=[pl.BlockSpec((CHUNK, MAIN_WIDTH), lambda b, c: (b * nc + c, 0)),
                  pl.BlockSpec((CHUNK, LANES), lambda b, c: (b * nc + c, 0)),
                  full((len(POOL_WINDOWS), POOL_GROUP_DIM, POOL_GROUP_DIM)),
                  full((1, POOL_WIDTH)),
                  full((SSD_CONV, XBC_WIDTH)),
                  full((1, XBC_WIDTH)),
                  full((1, LANES)), full((1, LANES)),
                  full((1, SSD_WIDTH)), full((1, SSD_WIDTH))],
        out_specs=[pl.BlockSpec((CHUNK, D_MODEL), lambda b, c: (b * nc + c, 0)),
                   pl.BlockSpec((None, D_STATE, SSD_WIDTH), lambda b, c: (b, 0, 0))],
        out_shape=[jax.ShapeDtypeStruct((TP, D_MODEL), BF16),
                   jax.ShapeDtypeStruct((BATCH, D_STATE, SSD_WIDTH), F32)],
        scratch_shapes=[pltpu.VMEM((8 + CHUNK, XBC_WIDTH), F32),
                        pltpu.VMEM((16 + CHUNK, POOL_WIDTH), F32),
                        pltpu.VMEM((D_STATE, SSD_WIDTH), F32)],
        compiler_params=_cparams(("arbitrary", "arbitrary")),
        name="mix_prompt",
    )(proj, dtp, lw["pool_w"], lw["pool_scale"], lw["conv_w"], lw["conv_b"], lw["dt_bias"], lw["a_log"],
      lw["d_skip_e"], lw["ssd_norm"])


SAMPLE_BT = 8


def _mixs_kernel(p0, p1, p2, p3, d0, d1, d2, d3, sconv_ref, spool_ref, sssm_ref,
                 poolw_ref, pscale_ref, cw_ref, cb_ref, dtb_ref, alog_ref, dskip_ref, ng_ref,
                 mix_ref, nconv_ref, nssm_ref, c_scr, b_scr, xw_scr, yoff_scr):
    bt = SAMPLE_BT
    P = (p0, p1, p2, p3)
    Dt = (d0, d1, d2, d3)
    L = DEC_SEQ

    fullp = [spool_ref[k] for k in range(POOL_BUF)] + [P[t][:, 0:POOL_WIDTH] for t in range(L)]
    for g, w in enumerate(POOL_WINDOWS):
        gl = slice(g * POOL_GROUP_DIM, (g + 1) * POOL_GROUP_DIM)
        ds = []
        for t in range(L):
            s = fullp[POOL_BUF + t][:, gl]
            for k in range(1, w):
                s = s + fullp[POOL_BUF + t - k][:, gl]
            cnt = float(min(PAST_LEN + t + 1, w))
            ds.append(s / cnt - fullp[POOL_BUF + t][:, gl])
        o = jnp.dot(jnp.concatenate(ds, axis=0).astype(BF16), poolw_ref[g].astype(BF16),
                    preferred_element_type=F32) * pscale_ref[:, gl]
        for t in range(L):
            mix_ref[t, :, gl] = o[t * bt:(t + 1) * bt]

    fullc = [sconv_ref[k] for k in range(SSD_CONV - 1)] + [P[t][:, POOL_WIDTH + SSD_WIDTH:MAIN_WIDTH] for t in range(L)]
    for k in range(SSD_CONV - 1):
        nconv_ref[k] = fullc[L + k]
    xs, Bm, Cm = [], [], []
    for t in range(L):
        acc = cb_ref[...]
        for i in range(SSD_CONV):
            acc = acc + cw_ref[i:i + 1, :] * fullc[t + i]
        xc = _silu(acc)
        xs.append(xc[:, 0:SSD_WIDTH])
        Bm.append(xc[:, SSD_WIDTH:SSD_WIDTH + SSD_GROUPS * D_STATE])
        Cm.append(xc[:, SSD_WIDTH + SSD_GROUPS * D_STATE:XBC_WIDTH])

    A = -jnp.exp(alog_ref[...])
    dts = [_softplus(Dt[t][...] + dtb_ref[...]) for t in range(L)]
    cs = []
    for t in range(L):
        dA = dts[t] * A
        cs.append(dA if t == 0 else cs[t - 1] + dA)
    dec24 = jnp.exp(cs[L - 1])
    expcs_e = [_expand_heads(jnp.exp(cs[t]), bt) for t in range(L)]
    xdt = [xs[t] * _expand_heads(dts[t], bt) for t in range(L)]

    ydiag = []
    for t in range(L):
        acc = None
        for s in range(t + 1):
            cb_parts = []
            for g in range(SSD_GROUPS):
                sl = slice(g * D_STATE, (g + 1) * D_STATE)
                r = jnp.sum(Cm[t][:, sl] * Bm[s][:, sl], axis=-1, keepdims=True)
                cb_parts.append(jnp.broadcast_to(r, (bt, GROUP_COLS)))
            coef = jnp.concatenate(cb_parts, axis=1)
            if s < t:
                coef = coef * _expand_heads(jnp.exp(cs[t] - cs[s]), bt)
            term = coef * xdt[s]
            acc = term if acc is None else acc + term
        ydiag.append(acc)

    for t in range(L):
        c_scr[t] = Cm[t]
        b_scr[t] = Bm[t]
        xw_scr[t] = xdt[t] * _expand_heads(jnp.exp(cs[L - 1] - cs[t]), bt)
    zpad_s = jnp.zeros((8 - L, D_STATE), F32)
    zpad_x = jnp.zeros((8 - L, GROUP_COLS), F32)
    for b in range(bt):
        for g in range(SSD_GROUPS):
            sl = slice(g * D_STATE, (g + 1) * D_STATE)
            gs = slice(g * GROUP_COLS, (g + 1) * GROUP_COLS)
            C8 = jnp.concatenate([c_scr[t, b:b + 1, sl] for t in range(L)] + [zpad_s], axis=0)
            B8 = jnp.concatenate([b_scr[t, b:b + 1, sl] for t in range(L)] + [zpad_s], axis=0)
            X8 = jnp.concatenate([xw_scr[t, b:b + 1, gs] for t in range(L)] + [zpad_x], axis=0)
            h0 = sssm_ref[b, SSD_HPG * g:SSD_HPG * (g + 1)].reshape(GROUP_COLS, D_STATE)
            GT = lax.dot_general(C8.astype(BF16), h0.astype(BF16), (((1,), (1,)), ((), ())),
                                 preferred_element_type=F32)
            U = lax.dot_general(X8, B8, (((0,), (0,)), ((), ())), preferred_element_type=F32)
            for r in range(SSD_HPG):
                h = SSD_HPG * g + r
                dec = jnp.broadcast_to(dec24[b:b + 1, h:h + 1], (SSD_HEAD_DIM, D_STATE))
                rs = slice(r * SSD_HEAD_DIM, (r + 1) * SSD_HEAD_DIM)
                nssm_ref[b, h] = h0[rs] * dec + U[rs]
            for t in range(L):
                yoff_scr[t, b:b + 1, gs] = GT[t:t + 1, :]

    for t in range(L):
        y = ydiag[t] + expcs_e[t] * yoff_scr[t] + dskip_ref[...] * xs[t]
        z = P[t][:, POOL_WIDTH:POOL_WIDTH + SSD_WIDTH]
        mix_ref[t, :, POOL_WIDTH:D_MODEL] = _gated_norm(y, z, ng_ref[...])


def _mix_sample(proj, dtp, sconv_t, spool_t, sssm, lw, l):
    bt = SAMPLE_BT
    nb = DEC_BATCH // bt
    base = TP // bt

    def pspec(t, width):
        return pl.BlockSpec((bt, width), lambda i, t=t: (base + t * nb + i, 0))

    full = lambda shape: pl.BlockSpec(shape, lambda i: (0,) * len(shape))
    return pl.pallas_call(
        _mixs_kernel,
        grid=(nb,),
        in_specs=[pspec(t, MAIN_WIDTH) for t in range(DEC_SEQ)] + [pspec(t, LANES) for t in range(DEC_SEQ)] + [
            pl.BlockSpec((None, SSD_CONV - 1, bt, XBC_WIDTH), lambda i: (l, 0, i, 0)),
            pl.BlockSpec((None, POOL_BUF, bt, POOL_WIDTH), lambda i: (l, 0, i, 0)),
            pl.BlockSpec((None, bt, SSD_HEADS, SSD_HEAD_DIM, D_STATE), lambda i: (l, i, 0, 0, 0)),
            full((len(POOL_WINDOWS), POOL_GROUP_DIM, POOL_GROUP_DIM)),
            full((1, POOL_WIDTH)),
            full((SSD_CONV, XBC_WIDTH)),
            full((1, XBC_WIDTH)),
            full((1, LANES)), full((1, LANES)),
            full((1, SSD_WIDTH)), full((1, SSD_WIDTH))],
        out_specs=[pl.BlockSpec((DEC_SEQ, bt, D_MODEL), lambda i: (0, i, 0)),
                   pl.BlockSpec((SSD_CONV - 1, bt, XBC_WIDTH), lambda i: (0, i, 0)),
                   pl.BlockSpec((bt, SSD_HEADS, SSD_HEAD_DIM, D_STATE), lambda i: (i, 0, 0, 0))],
        out_shape=[jax.ShapeDtypeStruct((DEC_SEQ, DEC_BATCH, D_MODEL), F32),
                   jax.ShapeDtypeStruct((SSD_CONV - 1, DEC_BATCH, XBC_WIDTH), F32),
                   jax.ShapeDtypeStruct((DEC_BATCH, SSD_HEADS, SSD_HEAD_DIM, D_STATE), F32)],
        scratch_shapes=[pltpu.VMEM((DEC_SEQ, bt, SSD_GROUPS * D_STATE), F32),
                        pltpu.VMEM((DEC_SEQ, bt, SSD_GROUPS * D_STATE), F32),
                        pltpu.VMEM((DEC_SEQ, bt, SSD_WIDTH), F32),
                        pltpu.VMEM((DEC_SEQ, bt, SSD_WIDTH), F32)],
        compiler_params=_cparams(("arbitrary",)),
        name="mix_sample",
    )(proj, proj, proj, proj, dtp, dtp, dtp, dtp, sconv_t, spool_t, sssm,
      lw["pool_w"], lw["pool_scale"], lw["conv_w"], lw["conv_b"], lw["dt_bias"], lw["a_log"],
      lw["d_skip_e"], lw["ssd_norm"])


def _resid_epilogue(acc, x_ref, gp, gs, ng_ref, mods, xo_ref, ho_ref, is_s, tm):
    xn = x_ref[...] + _rowmod(gp, gs, is_s, tm) * acc
    xo_ref[...] = xn
    if mods is None:
        ms = jnp.mean(xn * xn, axis=-1, keepdims=True)
        ho_ref[...] = (xn * lax.rsqrt(ms + EPS) * ng_ref[...]).astype(ho_ref.dtype)
    else:
        scp, scs, shp, shs = mods
        sc = _rowmod(scp, scs, is_s, tm)
        sh = _rowmod(shp, shs, is_s, tm)
        ho_ref[...] = _rms_mod(xn, ng_ref[...], sc, sh).astype(ho_ref.dtype)


def _outproj_kernel(ap_ref, as_ref, w_ref, x_ref, gp, gs, ng_ref, *rest, npt, tm, final):
    mods = None if final else rest[:4]
    xo_ref, ho_ref = rest[-2:]
    is_s = pl.program_id(0) >= npt

    @pl.when(jnp.logical_not(is_s))
    def _():
        acc = jnp.dot(ap_ref[...], w_ref[...], preferred_element_type=F32)
        _resid_epilogue(acc, x_ref, gp, gs, ng_ref, mods, xo_ref, ho_ref, is_s, tm)

    @pl.when(is_s)
    def _():
        acc = jnp.dot(as_ref[...].astype(BF16), w_ref[...], preferred_element_type=F32)
        _resid_epilogue(acc, x_ref, gp, gs, ng_ref, mods, xo_ref, ho_ref, is_s, tm)


def _down_kernel(a_ref, w_ref, x_ref, gp, gs, ng_ref, *rest, npt, tm, final):
    mods = None if final else rest[:4]
    xo_ref, ho_ref = rest[-2:]
    is_s = pl.program_id(0) >= npt
    acc = jnp.dot(a_ref[...], w_ref[...], preferred_element_type=F32)
    _resid_epilogue(acc, x_ref, gp, gs, ng_ref, mods, xo_ref, ho_ref, is_s, tm)


def _resid_proj(a_list, w, x, gain, modp, mods, l, gate_k, next_mod, h_dtype):
    tm = 256
    npt = TP // tm
    row = lambda i: i
    kdim = w.shape[0]
    gp, gs = _mod_specs(l, gate_k, tm, row)
    in_specs = []
    if len(a_list) == 2:
        in_specs += [pl.BlockSpec((tm, kdim), lambda i: (jnp.minimum(i, npt - 1), 0)),
                     pl.BlockSpec((tm, kdim), lambda i: (jnp.maximum(i - npt, 0), 0))]
        kern = _outproj_kernel
    else:
        in_specs += [pl.BlockSpec((tm, kdim), lambda i: (i, 0))]
        kern = _down_kernel
    in_specs += [pl.BlockSpec((kdim, D_MODEL), lambda i: (0, 0), pipeline_mode=pl.Buffered(1)),
                 pl.BlockSpec((tm, D_MODEL), lambda i: (i, 0)),
                 gp, gs,
                 pl.BlockSpec((1, D_MODEL), lambda i: (0, 0))]
    args = list(a_list) + [w, x, modp, mods, gain.reshape(1, D_MODEL)]
    final = next_mod is None
    if not final:
        nl, shift_k, scale_k = next_mod
        scp, scs = _mod_specs(nl, scale_k, tm, row)
        shp, shs = _mod_specs(nl, shift_k, tm, row)
        in_specs += [scp, scs, shp, shs]
        args += [modp, mods, modp, mods]
    return pl.pallas_call(
        functools.partial(kern, npt=npt, tm=tm, final=final),
        grid=(T // tm,),
        in_specs=in_specs,
        out_specs=[pl.BlockSpec((tm, D_MODEL), lambda i: (i, 0)),
                   pl.BlockSpec((tm, D_MODEL), lambda i: (i, 0))],
        out_shape=[jax.ShapeDtypeStruct((T, D_MODEL), F32),
                   jax.ShapeDtypeStruct((T, D_MODEL), h_dtype)],
        compiler_params=_cparams(("arbitrary",)),
        name="resid_proj",
    )(*args)


def _up_kernel(h_ref, wg_ref, wv_ref, cw_ref, cb_ref, sf_ref, act_ref, nfp_ref, nfs_ref, gpad, *, npt, tpb, tm, tn):
    m = pl.program_id(1)
    h = h_ref[...]
    gate = jnp.dot(h, wg_ref[...], preferred_element_type=F32)
    val = jnp.dot(h, wv_ref[...], preferred_element_type=F32)
    w0 = cw_ref[0:1, :]
    w1 = cw_ref[1:2, :]
    w2 = cw_ref[2:3, :]
    base = cb_ref[...] + w2 * gate

    @pl.when(m < npt)
    def _():
        @pl.when(m % tpb == 0)
        def _():
            gpad[0:8, :] = jnp.zeros((8, tn), F32)

        gpad[8:8 + tm, :] = gate
        gc = base + w0 * gpad[6:6 + tm, :] + w1 * gpad[7:7 + tm, :]
        act_ref[...] = (_silu(gc) * val).astype(act_ref.dtype)
        gpad[0:8, :] = gpad[tm:tm + 8, :]

        @pl.when(m % tpb == tpb - 1)
        def _():
            nfp_ref[...] = gate[tm - (FFN_CONV - 1):tm, :]

    @pl.when(m >= npt)
    def _():
        p0 = sf_ref[0]
        p1 = sf_ref[1]
        g = [gate[t * DEC_BATCH:(t + 1) * DEC_BATCH] for t in range(DEC_SEQ)]
        gm1 = jnp.concatenate([p1, g[0], g[1], g[2]], axis=0)
        gm2 = jnp.concatenate([p0, p1, g[0], g[1]], axis=0)
        gc = base + w0 * gm2 + w1 * gm1
        act_ref[...] = (_silu(gc) * val).astype(act_ref.dtype)
        nfs_ref[0] = g[2]
        nfs_ref[1] = g[3]


def _ffn_up(h2, wg, wv, conv_w, conv_b, sffn_t, l):
    tm, tn = TS, 512
    npt = TP // tm
    tpb = SEQ // tm
    nn = pl.cdiv(D_FF, tn)
    return pl.pallas_call(
        functools.partial(_up_kernel, npt=npt, tpb=tpb, tm=tm, tn=tn),
        grid=(nn, T // tm),
        in_specs=[pl.BlockSpec((tm, D_MODEL), lambda j, m: (m, 0)),
                  pl.BlockSpec((D_MODEL, tn), lambda j, m: (0, j)),
                  pl.BlockSpec((D_MODEL, tn), lambda j, m: (0, j)),
                  pl.BlockSpec((FFN_CONV, tn), lambda j, m: (0, j)),
                  pl.BlockSpec((1, tn), lambda j, m: (0, j)),
                  pl.BlockSpec((None, FFN_CONV - 1, DEC_BATCH, tn), lambda j, m: (l, 0, 0, j))],
        out_specs=[pl.BlockSpec((tm, tn), lambda j, m: (m, j)),
                   pl.BlockSpec((None, FFN_CONV - 1, tn), lambda j, m: (jnp.minimum(m // tpb, BATCH - 1), 0, j)),
                   pl.BlockSpec((FFN_CONV - 1, DEC_BATCH, tn), lambda j, m: (0, 0, j))],
        out_shape=[jax.ShapeDtypeStruct((T, D_FF), BF16),
                   jax.ShapeDtypeStruct((BATCH, FFN_CONV - 1, D_FF), F32),
                   jax.ShapeDtypeStruct((FFN_CONV - 1, DEC_BATCH, D_FF), F32)],
        scratch_shapes=[pltpu.VMEM((8 + tm, tn), F32)],
        compiler_params=_cparams(("arbitrary", "arbitrary")),
        name="ffn_up",
    )(h2, wg, wv, conv_w, conv_b.reshape(1, D_FF), sffn_t)


def _pad_lanes(v):
    return jnp.pad(v, ((0, 0), (0, LANES - v.shape[-1])))


def kernel(x_prompt, x_sample, c_prompt, c_sample, state_pool, state_conv, state_ssm, state_ffn, norm1, norm2, w_ada, b_ada, w_in, pool_w, pool_scale, conv_w, conv_b, dt_bias, a_log, d_skip, ssd_norm, w_out, w_up, ffn_conv_w, ffn_conv_b, w_down, norm_f):
    x = jnp.concatenate([x_prompt.reshape(TP, D_MODEL),
                         jnp.swapaxes(x_sample, 0, 1).reshape(TS, D_MODEL)], axis=0)
    c_all = jnp.concatenate([c_sample, c_prompt, jnp.zeros((4, D_MODEL), F32)], axis=0)
    mods, modp = _ada(c_all, w_ada, b_ada)
    modp = modp.reshape(DEPTH, BATCH, 1, 6 * D_MODEL)

    spool_t = jnp.swapaxes(state_pool, 1, 2)
    sconv_t = jnp.swapaxes(state_conv, 1, 2)
    sffn_t = jnp.swapaxes(state_ffn, 1, 2)

    w_main = w_in[:, :, :MAIN_WIDTH].astype(BF16)
    w_dt = jnp.pad(w_in[:, :, MAIN_WIDTH:], ((0, 0), (0, 0), (0, LANES - SSD_HEADS))).astype(BF16)
    w_out_b = w_out.astype(BF16)
    w_gate = w_up[:, :, :D_FF].astype(BF16)
    w_val = w_up[:, :, D_FF:].astype(BF16)
    w_down_b = w_down.astype(BF16)

    d_skip_e = jnp.repeat(d_skip, SSD_HEAD_DIM, axis=-1)

    outs = {k: [] for k in ("pp", "pc", "ps", "pf", "sp", "sc", "ss", "sf")}
    h = _norm0(x, norm1[0], modp, mods, 0)
    y = None
    for l in range(DEPTH):
        lw = {"pool_w": pool_w[l], "pool_scale": pool_scale[l].reshape(1, POOL_WIDTH),
              "conv_w": conv_w[l], "conv_b": conv_b[l].reshape(1, XBC_WIDTH),
              "dt_bias": _pad_lanes(dt_bias[l].reshape(1, SSD_HEADS)),
              "a_log": _pad_lanes(a_log[l].reshape(1, SSD_HEADS)),
              "d_skip_e": d_skip_e[l].reshape(1, SSD_WIDTH),
              "ssd_norm": ssd_norm[l].reshape(1, SSD_WIDTH)}
        proj, dtp = _inproj(h, w_main[l], w_dt[l])
        mix_p, st_p = _mix_prompt(proj, dtp, lw)
        mix_s, nconv_s, nssm_s = _mix_sample(proj, dtp, sconv_t, spool_t, state_ssm, lw, l)
        x1, h2 = _resid_proj([mix_p, mix_s.reshape(TS, D_MODEL)], w_out_b[l], x, norm2[l], modp, mods, l, 2,
                             (l, 3, 4), BF16)
        act, nffn_p, nffn_s = _ffn_up(h2, w_gate[l], w_val[l], ffn_conv_w[l], ffn_conv_b[l], sffn_t, l)
        if l + 1 < DEPTH:
            x, h = _resid_proj([act], w_down_b[l], x1, norm1[l + 1], modp, mods, l, 5, (l + 1, 0, 1), BF16)
        else:
            x, y = _resid_proj([act], w_down_b[l], x1, norm_f, modp, mods, l, 5, None, F32)

        pj = proj[:TP].reshape(BATCH, SEQ, MAIN_WIDTH)
        outs["pp"].append(pj[:, SEQ - POOL_BUF:, :POOL_WIDTH])
        outs["pc"].append(pj[:, SEQ - (SSD_CONV - 1):, POOL_WIDTH + SSD_WIDTH:])
        outs["ps"].append(jnp.transpose(st_p.reshape(BATCH, D_STATE, SSD_HEADS, SSD_HEAD_DIM), (0, 2, 3, 1)))
        outs["pf"].append(nffn_p)
        u_s = jnp.swapaxes(proj[TP:, :POOL_WIDTH].reshape(DEC_SEQ, DEC_BATCH, POOL_WIDTH), 0, 1)
        outs["sp"].append(jnp.concatenate([state_pool[l][:, DEC_SEQ:], u_s], axis=1))
        outs["sc"].append(jnp.swapaxes(nconv_s, 0, 1))
        outs["ss"].append(nssm_s)
        outs["sf"].append(jnp.swapaxes(nffn_s, 0, 1))

    y_prompt = y[:TP].reshape(BATCH, SEQ, D_MODEL)
    y_sample = jnp.swapaxes(y[TP:].reshape(DEC_SEQ, DEC_BATCH, D_MODEL), 0, 1)
    return (y_prompt, y_sample,
            jnp.stack(outs["pp"]), jnp.stack(outs["pc"]), jnp.stack(outs["ps"]), jnp.stack(outs["pf"]),
            jnp.stack(outs["sp"]), jnp.stack(outs["sc"]), jnp.stack(outs["ss"]), jnp.stack(outs["sf"]))
```

```python
import functools

import jax
import jax.numpy as jnp
from jax import lax
from jax.experimental import pallas as pl
from jax.experimental.pallas import tpu as pltpu

F32 = jnp.float32
BF16 = jnp.bfloat16

D_MODEL = 2048
BATCH = 4
SEQ = 2048
DEPTH = 4
DEC_BATCH = 128
DEC_SEQ = 4
PAST_LEN = 16384

POOL_WIDTH = 512
POOL_WINDOWS = (2, 4, 8, 16)
POOL_GROUP_DIM = 128
POOL_BUF = 15
SSD_WIDTH = 1536
SSD_HEAD_DIM = 64
SSD_HEADS = 24
SSD_GROUPS = 4
SSD_HPG = 6
D_STATE = 128
SSD_CONV = 4
XBC_WIDTH = 2560
MAIN_WIDTH = POOL_WIDTH + SSD_WIDTH + XBC_WIDTH
D_FF = 5504
FFN_CONV = 3
EPS = 1e-6

TP = BATCH * SEQ
TS = DEC_SEQ * DEC_BATCH
T = TP + TS
GROUP_COLS = SSD_HPG * SSD_HEAD_DIM
LANES = 128
CHUNK = 128

VMEM_LIMIT = 56 * 1024 * 1024


def _silu(x):
    return x * jax.nn.sigmoid(x)


def _softplus(x):
    return jnp.maximum(x, 0.0) + jnp.log1p(jnp.exp(-jnp.abs(x)))


def _cparams(sem):
    return pltpu.CompilerParams(dimension_semantics=sem, vmem_limit_bytes=VMEM_LIMIT)


def _rowmod(p_ref, s_ref, is_sample, tm):
    s = s_ref[...]
    if tm > DEC_BATCH:
        s = jnp.concatenate([s] * (tm // DEC_BATCH), axis=0)
    return jnp.where(is_sample, s, p_ref[...])


def _rms_mod(x, gain, sc, sh):
    ms = jnp.mean(x * x, axis=-1, keepdims=True)
    y = x * lax.rsqrt(ms + EPS) * gain
    return y * (1.0 + sc) + sh


def _ada_kernel(c_ref, w_ref, b_ref, os_ref, op_ref):
    a = _silu(c_ref[...]).astype(BF16)
    m = jnp.dot(a, w_ref[...].astype(BF16), preferred_element_type=F32) + b_ref[...]
    os_ref[...] = m[:DEC_BATCH]
    op_ref[...] = m[DEC_BATCH:DEC_BATCH + BATCH]


def _ada(c_all, w_ada, b_ada):
    tn = 1024
    rows = c_all.shape[0]
    return pl.pallas_call(
        _ada_kernel,
        grid=(DEPTH, 6 * D_MODEL // tn),
        in_specs=[
            pl.BlockSpec((rows, D_MODEL), lambda l, j: (0, 0)),
            pl.BlockSpec((None, D_MODEL, tn), lambda l, j: (l, 0, j)),
            pl.BlockSpec((None, 1, tn), lambda l, j: (l, 0, j)),
        ],
        out_specs=[
            pl.BlockSpec((None, DEC_BATCH, tn), lambda l, j: (l, 0, j)),
            pl.BlockSpec((None, BATCH, tn), lambda l, j: (l, 0, j)),
        ],
        out_shape=[
            jax.ShapeDtypeStruct((DEPTH, DEC_BATCH, 6 * D_MODEL), F32),
            jax.ShapeDtypeStruct((DEPTH, BATCH, 6 * D_MODEL), F32),
        ],
        compiler_params=_cparams(("arbitrary", "arbitrary")),
        name="ada",
    )(c_all, w_ada, b_ada.reshape(DEPTH, 1, 6 * D_MODEL))


def _mod_specs(l, k, tm, row_of):
    tpb = SEQ // tm

    def pmap(*ids):
        return (l, jnp.minimum(row_of(*ids) // tpb, BATCH - 1), 0, k)

    def smap(*ids):
        return (l, 0, k)

    return (pl.BlockSpec((None, None, 1, D_MODEL), pmap), pl.BlockSpec((None, DEC_BATCH, D_MODEL), smap))


def _norm_kernel(x_ref, g_ref, scp, scs, shp, shs, o_ref, *, npt, tm):
    is_s = pl.program_id(0) >= npt
    sc = _rowmod(scp, scs, is_s, tm)
    sh = _rowmod(shp, shs, is_s, tm)
    o_ref[...] = _rms_mod(x_ref[...], g_ref[...], sc, sh).astype(o_ref.dtype)


def _norm0(x, gain, modp, mods, l):
    tm = 256
    npt = TP // tm
    row = lambda i: i
    scp, scs = _mod_specs(l, 1, tm, row)
    shp, shs = _mod_specs(l, 0, tm, row)
    return pl.pallas_call(
        functools.partial(_norm_kernel, npt=npt, tm=tm),
        grid=(T // tm,),
        in_specs=[pl.BlockSpec((tm, D_MODEL), lambda i: (i, 0)),
                  pl.BlockSpec((1, D_MODEL), lambda i: (0, 0)),
                  scp, scs, shp, shs],
        out_specs=pl.BlockSpec((tm, D_MODEL), lambda i: (i, 0)),
        out_shape=jax.ShapeDtypeStruct((T, D_MODEL), BF16),
        compiler_params=_cparams(("arbitrary",)),
        name="norm0",
    )(x, gain.reshape(1, D_MODEL), modp, mods, modp, mods)


def _inproj_kernel(h_ref, w_ref, wdt_ref, o_ref, dt_ref, wb, wdtb):
    j = pl.program_id(0)
    m = pl.program_id(1)

    @pl.when(m == 0)
    def _():
        wb[...] = w_ref[...].astype(BF16)

    h = h_ref[...]
    o_ref[...] = jnp.dot(h, wb[...], preferred_element_type=F32)

    @pl.when(j == 0)
    def _():
        @pl.when(m == 0)
        def _():
            lane = lax.broadcasted_iota(jnp.int32, (1, LANES), 1)
            wdtb[...] = jnp.where(lane < SSD_HEADS, wdt_ref[...], 0.0).astype(BF16)

        dt_ref[...] = jnp.dot(h, wdtb[...], preferred_element_type=F32)

    @pl.when(j != 0)
    def _():
        dt_ref[...] = jnp.zeros(dt_ref.shape, F32)


def _inproj(h, w_in, l):
    tm, tn = 512, 1536
    nm = T // tm
    return pl.pallas_call(
        _inproj_kernel,
        grid=(MAIN_WIDTH // tn, nm),
        in_specs=[pl.BlockSpec((tm, D_MODEL), lambda j, m: (m, 0)),
                  pl.BlockSpec((None, D_MODEL, tn), lambda j, m: (l, 0, j)),
                  pl.BlockSpec((None, D_MODEL, LANES), lambda j, m: (l, 0, MAIN_WIDTH // LANES))],
        out_specs=[pl.BlockSpec((tm, tn), lambda j, m: (m, j)),
                   pl.BlockSpec((tm, LANES), lambda j, m: (jnp.where(j == 0, m, nm), 0))],
        out_shape=[jax.ShapeDtypeStruct((T, MAIN_WIDTH), F32),
                   jax.ShapeDtypeStruct((T + tm, LANES), F32)],
        scratch_shapes=[pltpu.VMEM((D_MODEL, tn), BF16), pltpu.VMEM((D_MODEL, LANES), BF16)],
        compiler_params=_cparams(("arbitrary", "arbitrary")),
        name="inproj",
    )(h, w_in, w_in)


def _expand_heads(v, rows):
    lo = lax.broadcasted_iota(jnp.int32, (rows, LANES), 1) < SSD_HEAD_DIM
    parts = []
    for j in range(SSD_HEADS // 2):
        a = jnp.broadcast_to(v[:, 2 * j:2 * j + 1], (rows, LANES))
        b = jnp.broadcast_to(v[:, 2 * j + 1:2 * j + 2], (rows, LANES))
        parts.append(jnp.where(lo, a, b))
    return jnp.concatenate(parts, axis=1)


def _split3(x):
    a = x.astype(BF16)
    r = x - a.astype(F32)
    b = r.astype(BF16)
    c = (r - b.astype(F32)).astype(BF16)
    return a, b, c


def _gated_norm(y, z, gain):
    y = y * _silu(z)
    ms = jnp.mean(y * y, axis=-1, keepdims=True)
    return y * lax.rsqrt(ms + EPS) * gain


def _mixp_kernel(proj_ref, dt_ref, poolw_ref, pscale_ref, cw_ref, cb_ref, dtb_ref, alog_ref, dskip_ref, ng_ref,
                 mix_ref, st_ref, xpad, upad, hT):
    c = pl.program_id(1)
    Q = CHUNK

    @pl.when(c == 0)
    def _():
        xpad[0:8, :] = jnp.zeros((8, XBC_WIDTH), F32)
        upad[0:16, :] = jnp.zeros((16, POOL_WIDTH), F32)
        hT[...] = jnp.zeros(hT.shape, F32)

    u = proj_ref[:, 0:POOL_WIDTH]
    upad[16:16 + Q, :] = u
    pos = c * Q + lax.broadcasted_iota(jnp.int32, (Q, 1), 0)
    outs = []
    for g, w in enumerate(POOL_WINDOWS):
        lo = g * POOL_GROUP_DIM
        ug = u[:, lo:lo + POOL_GROUP_DIM]
        s = ug
        for k in range(1, w):
            s = s + upad[16 - k:16 - k + Q, lo:lo + POOL_GROUP_DIM]
        cnt = jnp.minimum(pos + 1, w).astype(F32)
        d = s / cnt - ug
        outs.append(jnp.dot(d.astype(BF16), poolw_ref[g].astype(BF16), preferred_element_type=F32))
    mix_ref[:, 0:POOL_WIDTH] = (jnp.concatenate(outs, axis=1) * pscale_ref[...]).astype(mix_ref.dtype)
    upad[0:16, :] = upad[Q:Q + 16, :]

    xbc = proj_ref[:, POOL_WIDTH + SSD_WIDTH:MAIN_WIDTH]
    xpad[8:8 + Q, :] = xbc
    acc = cb_ref[...] + cw_ref[3:4, :] * xbc
    for i in range(SSD_CONV - 1):
        acc = acc + cw_ref[i:i + 1, :] * xpad[5 + i:5 + i + Q, :]
    xpad[0:8, :] = xpad[Q:Q + 8, :]
    xc = _silu(acc)
    xs = xc[:, 0:SSD_WIDTH]
    Bb = xc[:, SSD_WIDTH:SSD_WIDTH + SSD_GROUPS * D_STATE].astype(BF16)
    Cb = xc[:, SSD_WIDTH + SSD_GROUPS * D_STATE:XBC_WIDTH].astype(BF16)

    dt = _softplus(dt_ref[...] + dtb_ref[...])
    dA = dt * (-jnp.exp(alog_ref[...]))
    row = lax.broadcasted_iota(jnp.int32, (Q, Q), 0)
    col = lax.broadcasted_iota(jnp.int32, (Q, Q), 1)
    tri = row >= col
    tri_lo = jnp.where(tri, 1.0, 0.0).astype(BF16)
    tri_up = jnp.where(row <= col, 1.0, 0.0).astype(BF16)
    pieces = _split3(dA)
    cs = sum(jnp.dot(tri_lo, p, preferred_element_type=F32) for p in pieces)
    csT = sum(lax.dot_general(p, tri_up, (((0,), (0,)), ((), ())), preferred_element_type=F32)
              for p in pieces)
    cs_last = cs[Q - 1:Q, :]
    expcs_e = _expand_heads(jnp.exp(cs), Q)
    dte_e = _expand_heads(jnp.exp(cs_last - cs), Q)
    xdt = xs * _expand_heads(dt, Q)
    lane = lax.broadcasted_iota(jnp.int32, (1, SSD_WIDTH), 1)
    lo_mask = jnp.where((lane & (LANES - 1)) < SSD_HEAD_DIM, 1.0, 0.0)
    xdt_lo = (xdt * lo_mask).astype(BF16)
    xdt_hi = (xdt * (1.0 - lo_mask)).astype(BF16)
    xw_b = (xdt * dte_e).astype(BF16)
    decay_e = expcs_e[Q - 1:Q, :]
    h_prev = hT[...]
    h_prev_b = h_prev.astype(BF16)

    ys = []
    for g in range(SSD_GROUPS):
        gs = slice(g * GROUP_COLS, (g + 1) * GROUP_COLS)
        Cg = Cb[:, g * D_STATE:(g + 1) * D_STATE]
        Bg = Bb[:, g * D_STATE:(g + 1) * D_STATE]
        CB = lax.dot_general(Cg, Bg, (((1,), (1,)), ((), ())), preferred_element_type=F32)
        y_off = jnp.dot(Cg, h_prev_b[:, gs], preferred_element_type=F32)
        ST = lax.dot_general(Bg, xw_b[:, gs], (((0,), (0,)), ((), ())), preferred_element_type=F32)
        hT[:, gs] = h_prev[:, gs] * decay_e[:, gs] + ST
        parts = []
        for j in range(SSD_HPG // 2):
            h1 = SSD_HPG * g + 2 * j
            ps = slice(h1 * SSD_HEAD_DIM, h1 * SSD_HEAD_DIM + LANES)
            W = []
            for h in (h1, h1 + 1):
                seg = cs[:, h:h + 1] - csT[h:h + 1, :]
                W.append((CB * jnp.exp(jnp.where(tri, seg, -jnp.inf))).astype(BF16))
            parts.append(jnp.dot(W[0], xdt_lo[:, ps], preferred_element_type=F32)
                         + jnp.dot(W[1], xdt_hi[:, ps], preferred_element_type=F32))
        ys.append(jnp.concatenate(parts, axis=1) + y_off * expcs_e[:, gs])
    y = jnp.concatenate(ys, axis=1) + dskip_ref[...] * xs
    z = proj_ref[:, POOL_WIDTH:POOL_WIDTH + SSD_WIDTH]
    mix_ref[:, POOL_WIDTH:D_MODEL] = _gated_norm(y, z, ng_ref[...]).astype(mix_ref.dtype)

    @pl.when(c == pl.num_programs(1) - 1)
    def _():
        st_ref[...] = hT[...]


def _mix_prompt(proj, dtp, lw):
    nc = SEQ // CHUNK
    full = lambda shape: pl.BlockSpec(shape, lambda b, c: (0,) * len(shape))
    return pl.pallas_call(
        _mixp_kernel,
        grid=(BATCH, nc),
        in_specs=[pl.BlockSpec((CHUNK, MAIN_WIDTH), lambda b, c: (b * nc + c, 0)),
                  pl.BlockSpec((CHUNK, LANES), lambda b, c: (b * nc + c, 0)),
                  full((len(POOL_WINDOWS), POOL_GROUP_DIM, POOL_GROUP_DIM)),
                  full((1, POOL_WIDTH)),
                  full((SSD_CONV, XBC_WIDTH)),
                  full((1, XBC_WIDTH)),
                  full((1, LANES)), full((1, LANES)),
                  full((1, SSD_WIDTH)), full((1, SSD_WIDTH))],
        out_specs=[pl.BlockSpec((CHUNK, D_MODEL), lambda b, c: (b * nc + c, 0)),
                   pl.BlockSpec((None, D_STATE, SSD_WIDTH), lambda b, c: (b, 0, 0))],
        out_shape=[jax.ShapeDtypeStruct((TP, D_MODEL), BF16),
                   jax.ShapeDtypeStruct((BATCH, D_STATE, SSD_WIDTH), F32)],
        scratch_shapes=[pltpu.VMEM((8 + CHUNK, XBC_WIDTH), F32),
                        pltpu.VMEM((16 + CHUNK, POOL_WIDTH), F32),
                        pltpu.VMEM((D_STATE, SSD_WIDTH), F32)],
        compiler_params=_cparams(("arbitrary", "arbitrary")),
        name="mix_prompt",
    )(proj, dtp, lw["pool_w"], lw["pool_scale"], lw["conv_w"], lw["conv_b"], lw["dt_bias"], lw["a_log"],
      lw["d_skip_e"], lw["ssd_norm"])


SAMPLE_BT = 8


def _mixs_kernel(*refs, aliased):
    if aliased:
        refs = refs[1:]
    (p0, p1, p2, p3, d0, d1, d2, d3, sconv_ref, spool_ref, sssm_ref,
     poolw_ref, pscale_ref, cw_ref, cb_ref, dtb_ref, alog_ref, dskip_ref, ng_ref,
     mix_ref, nconv_ref, nssm_ref, c_scr, b_scr, xw_scr, yoff_scr) = refs
    bt = SAMPLE_BT
    P = (p0, p1, p2, p3)
    Dt = (d0, d1, d2, d3)
    L = DEC_SEQ

    fullp = [spool_ref[k] for k in range(POOL_BUF)] + [P[t][:, 0:POOL_WIDTH] for t in range(L)]
    for g, w in enumerate(POOL_WINDOWS):
        gl = slice(g * POOL_GROUP_DIM, (g + 1) * POOL_GROUP_DIM)
        ds = []
        for t in range(L):
            s = fullp[POOL_BUF + t][:, gl]
            for k in range(1, w):
                s = s + fullp[POOL_BUF + t - k][:, gl]
            cnt = float(min(PAST_LEN + t + 1, w))
            ds.append(s / cnt - fullp[POOL_BUF + t][:, gl])
        o = jnp.dot(jnp.concatenate(ds, axis=0).astype(BF16), poolw_ref[g].astype(BF16),
                    preferred_element_type=F32) * pscale_ref[:, gl]
        for t in range(L):
            mix_ref[t, :, gl] = o[t * bt:(t + 1) * bt]

    fullc = [sconv_ref[k] for k in range(SSD_CONV - 1)] + [P[t][:, POOL_WIDTH + SSD_WIDTH:MAIN_WIDTH] for t in range(L)]
    for k in range(SSD_CONV - 1):
        nconv_ref[k] = fullc[L + k]
    xs, Bm, Cm = [], [], []
    for t in range(L):
        acc = cb_ref[...]
        for i in range(SSD_CONV):
            acc = acc + cw_ref[i:i + 1, :] * fullc[t + i]
        xc = _silu(acc)
        xs.append(xc[:, 0:SSD_WIDTH])
        Bm.append(xc[:, SSD_WIDTH:SSD_WIDTH + SSD_GROUPS * D_STATE])
        Cm.append(xc[:, SSD_WIDTH + SSD_GROUPS * D_STATE:XBC_WIDTH])

    A = -jnp.exp(alog_ref[...])
    dts = [_softplus(Dt[t][...] + dtb_ref[...]) for t in range(L)]
    cs = []
    for t in range(L):
        dA = dts[t] * A
        cs.append(dA if t == 0 else cs[t - 1] + dA)
    dec24 = jnp.exp(cs[L - 1])
    expcs_e = [_expand_heads(jnp.exp(cs[t]), bt) for t in range(L)]
    xdt = [xs[t] * _expand_heads(dts[t], bt) for t in range(L)]

    ydiag = []
    for t in range(L):
        acc = None
        for s in range(t + 1):
            cb_parts = []
            for g in range(SSD_GROUPS):
                sl = slice(g * D_STATE, (g + 1) * D_STATE)
                r = jnp.sum(Cm[t][:, sl] * Bm[s][:, sl], axis=-1, keepdims=True)
                cb_parts.append(jnp.broadcast_to(r, (bt, GROUP_COLS)))
            coef = jnp.concatenate(cb_parts, axis=1)
            if s < t:
                coef = coef * _expand_heads(jnp.exp(cs[t] - cs[s]), bt)
            term = coef * xdt[s]
            acc = term if acc is None else acc + term
        ydiag.append(acc)

    for t in range(L):
        c_scr[t] = Cm[t]
        b_scr[t] = Bm[t]
        xw_scr[t] = xdt[t] * _expand_heads(jnp.exp(cs[L - 1] - cs[t]), bt)
    zpad_s = jnp.zeros((8 - L, D_STATE), F32)
    zpad_x = jnp.zeros((8 - L, GROUP_COLS), F32)
    for b in range(bt):
        for g in range(SSD_GROUPS):
            sl = slice(g * D_STATE, (g + 1) * D_STATE)
            gs = slice(g * GROUP_COLS, (g + 1) * GROUP_COLS)
            C8 = jnp.concatenate([c_scr[t, b:b + 1, sl] for t in range(L)] + [zpad_s], axis=0)
            B8 = jnp.concatenate([b_scr[t, b:b + 1, sl] for t in range(L)] + [zpad_s], axis=0)
            X8 = jnp.concatenate([xw_scr[t, b:b + 1, gs] for t in range(L)] + [zpad_x], axis=0)
            h0 = sssm_ref[b, SSD_HPG * g:SSD_HPG * (g + 1)].reshape(GROUP_COLS, D_STATE)
            GT = lax.dot_general(C8.astype(BF16), h0.astype(BF16), (((1,), (1,)), ((), ())),
                                 preferred_element_type=F32)
            U = lax.dot_general(X8, B8, (((0,), (0,)), ((), ())), preferred_element_type=F32)
            for r in range(SSD_HPG):
                h = SSD_HPG * g + r
                dec = jnp.broadcast_to(dec24[b:b + 1, h:h + 1], (SSD_HEAD_DIM, D_STATE))
                rs = slice(r * SSD_HEAD_DIM, (r + 1) * SSD_HEAD_DIM)
                nssm_ref[b, h] = h0[rs] * dec + U[rs]
            for t in range(L):
                yoff_scr[t, b:b + 1, gs] = GT[t:t + 1, :]

    for t in range(L):
        y = ydiag[t] + expcs_e[t] * yoff_scr[t] + dskip_ref[...] * xs[t]
        z = P[t][:, POOL_WIDTH:POOL_WIDTH + SSD_WIDTH]
        mix_ref[t, :, POOL_WIDTH:D_MODEL] = _gated_norm(y, z, ng_ref[...])


def _mix_sample(proj, dtp, sconv_t, spool_t, sssm, lw, l, nssm_stack):
    bt = SAMPLE_BT
    nb = DEC_BATCH // bt
    base = TP // bt
    aliased = nssm_stack is not None

    def pspec(t, width):
        return pl.BlockSpec((bt, width), lambda i, t=t: (base + t * nb + i, 0))

    full = lambda shape: pl.BlockSpec(shape, lambda i: (0,) * len(shape))
    return pl.pallas_call(
        functools.partial(_mixs_kernel, aliased=aliased),
        grid=(nb,),
        input_output_aliases={0: 2} if aliased else {},
        in_specs=([pl.BlockSpec(memory_space=pl.ANY)] if aliased else [])
        + [pspec(t, MAIN_WIDTH) for t in range(DEC_SEQ)] + [pspec(t, LANES) for t in range(DEC_SEQ)] + [
            pl.BlockSpec((None, SSD_CONV - 1, bt, XBC_WIDTH), lambda i: (l, 0, i, 0)),
            pl.BlockSpec((None, POOL_BUF, bt, POOL_WIDTH), lambda i: (l, 0, i, 0)),
            pl.BlockSpec((None, bt, SSD_HEADS, SSD_HEAD_DIM, D_STATE), lambda i: (l, i, 0, 0, 0)),
            full((len(POOL_WINDOWS), POOL_GROUP_DIM, POOL_GROUP_DIM)),
            full((1, POOL_WIDTH)),
            full((SSD_CONV, XBC_WIDTH)),
            full((1, XBC_WIDTH)),
            full((1, LANES)), full((1, LANES)),
            full((1, SSD_WIDTH)), full((1, SSD_WIDTH))],
        out_specs=[pl.BlockSpec((DEC_SEQ, bt, D_MODEL), lambda i: (0, i, 0)),
                   pl.BlockSpec((SSD_CONV - 1, bt, XBC_WIDTH), lambda i: (0, i, 0)),
                   pl.BlockSpec((None, bt, SSD_HEADS, SSD_HEAD_DIM, D_STATE), lambda i: (l, i, 0, 0, 0))],
        out_shape=[jax.ShapeDtypeStruct((DEC_SEQ, DEC_BATCH, D_MODEL), F32),
                   jax.ShapeDtypeStruct((SSD_CONV - 1, DEC_BATCH, XBC_WIDTH), F32),
                   jax.ShapeDtypeStruct((DEPTH, DEC_BATCH, SSD_HEADS, SSD_HEAD_DIM, D_STATE), F32)],
        scratch_shapes=[pltpu.VMEM((DEC_SEQ, bt, SSD_GROUPS * D_STATE), F32),
                        pltpu.VMEM((DEC_SEQ, bt, SSD_GROUPS * D_STATE), F32),
                        pltpu.VMEM((DEC_SEQ, bt, SSD_WIDTH), F32),
                        pltpu.VMEM((DEC_SEQ, bt, SSD_WIDTH), F32)],
        compiler_params=_cparams(("arbitrary",)),
        name="mix_sample",
    )(*([nssm_stack] if aliased else []),
      proj, proj, proj, proj, dtp, dtp, dtp, dtp, sconv_t, spool_t, sssm,
      lw["pool_w"], lw["pool_scale"], lw["conv_w"], lw["conv_b"], lw["dt_bias"], lw["a_log"],
      lw["d_skip_e"], lw["ssd_norm"])


def _resid_epilogue(acc, x_ref, gp, gs, ng_ref, mods, is_s, tm):
    xn = x_ref[...] + _rowmod(gp, gs, is_s, tm) * acc
    if mods is None:
        ms = jnp.mean(xn * xn, axis=-1, keepdims=True)
        return xn, xn * lax.rsqrt(ms + EPS) * ng_ref[...]
    scp, scs, shp, shs = mods
    return xn, _rms_mod(xn, ng_ref[...], _rowmod(scp, scs, is_s, tm), _rowmod(shp, shs, is_s, tm))


def _outproj_kernel(ap_ref, as_ref, w_ref, x_ref, gp, gs, ng_ref, scp, scs, shp, shs, xo_ref, ho_ref, *, npt, tm):
    is_s = pl.program_id(0) >= npt
    a = jnp.where(is_s, as_ref[...].astype(BF16), ap_ref[...])
    acc = jnp.dot(a, w_ref[...], preferred_element_type=F32)
    xn, hn = _resid_epilogue(acc, x_ref, gp, gs, ng_ref, (scp, scs, shp, shs), is_s, tm)
    xo_ref[...] = xn
    ho_ref[...] = hn.astype(ho_ref.dtype)


def _down_kernel(a_ref, w_ref, x_ref, gp, gs, ng_ref, scp, scs, shp, shs, xo_ref, ho_ref, *, npt, tm):
    is_s = pl.program_id(0) >= npt
    acc = jnp.dot(a_ref[...], w_ref[...], preferred_element_type=F32)
    xn, hn = _resid_epilogue(acc, x_ref, gp, gs, ng_ref, (scp, scs, shp, shs), is_s, tm)
    xo_ref[...] = xn
    ho_ref[...] = hn.astype(ho_ref.dtype)


def _final_kernel(a_ref, w_ref, x_ref, gp, gs, ng_ref, yp_ref, ys_ref, *, npt, tm):
    is_s = pl.program_id(0) >= npt
    acc = jnp.dot(a_ref[...], w_ref[...], preferred_element_type=F32)
    _, yn = _resid_epilogue(acc, x_ref, gp, gs, ng_ref, None, is_s, tm)

    @pl.when(jnp.logical_not(is_s))
    def _():
        yp_ref[...] = yn

    @pl.when(is_s)
    def _():
        ys_ref[...] = yn


def _resid_proj(a_list, w, x, gain, modp, mods, l, gate_k, next_mod):
    tm = 256
    npt = TP // tm
    row = lambda i: i
    kdim = w.shape[1]
    gp, gs = _mod_specs(l, gate_k, tm, row)
    if len(a_list) == 2:
        in_specs = [pl.BlockSpec((tm, kdim), lambda i: (jnp.minimum(i, npt - 1), 0)),
                    pl.BlockSpec((tm, kdim), lambda i: (jnp.maximum(i - npt, 0), 0))]
        kern = _outproj_kernel
    else:
        in_specs = [pl.BlockSpec((tm, kdim), lambda i: (i, 0))]
        kern = _down_kernel
    in_specs += [pl.BlockSpec((None, kdim, D_MODEL), lambda i: (l, 0, 0), pipeline_mode=pl.Buffered(1)),
                 pl.BlockSpec((tm, D_MODEL), lambda i: (i, 0)),
                 gp, gs,
                 pl.BlockSpec((1, D_MODEL), lambda i: (0, 0))]
    args = list(a_list) + [w, x, modp, mods, gain.reshape(1, D_MODEL)]
    if next_mod is None:
        kern = _final_kernel
        out_specs = [pl.BlockSpec((tm, D_MODEL), lambda i: (jnp.minimum(i, npt - 1), 0)),
                     pl.BlockSpec((tm, D_MODEL), lambda i: (jnp.maximum(i - npt, 0), 0))]
        out_shape = [jax.ShapeDtypeStruct((TP, D_MODEL), F32), jax.ShapeDtypeStruct((TS, D_MODEL), F32)]
    else:
        nl, shift_k, scale_k = next_mod
        scp, scs = _mod_specs(nl, scale_k, tm, row)
        shp, shs = _mod_specs(nl, shift_k, tm, row)
        in_specs += [scp, scs, shp, shs]
        args += [modp, mods, modp, mods]
        out_specs = [pl.BlockSpec((tm, D_MODEL), lambda i: (i, 0)),
                     pl.BlockSpec((tm, D_MODEL), lambda i: (i, 0))]
        out_shape = [jax.ShapeDtypeStruct((T, D_MODEL), F32), jax.ShapeDtypeStruct((T, D_MODEL), BF16)]
    return pl.pallas_call(
        functools.partial(kern, npt=npt, tm=tm),
        grid=(T // tm,),
        in_specs=in_specs,
        out_specs=out_specs,
        out_shape=out_shape,
        compiler_params=_cparams(("arbitrary",)),
        name="resid_proj",
    )(*args)


def _up_kernel(h_ref, wg_ref, wv0, wv1, wv2, wv3, cw_ref, cb_ref, sf_ref, act_ref, nfp_ref, nfs_ref, wgb, wvb, gpad,
               *, npt, tpb, tm, tn):
    m = pl.program_id(1)

    @pl.when(m == 0)
    def _():
        wgb[...] = wg_ref[...].astype(BF16)
        for k, r in enumerate((wv0, wv1, wv2, wv3)):
            wvb[:, k * LANES:(k + 1) * LANES] = r[...].astype(BF16)

    h = h_ref[...]
    is_s = m >= npt
    first = (m % tpb) == 0
    hw = tn // 2
    for half in range(2):
        cs = slice(half * hw, (half + 1) * hw)
        gate = jnp.dot(h, wgb[:, cs], preferred_element_type=F32)
        val = jnp.dot(h, wvb[:, cs], preferred_element_type=F32)
        gpad[0:8, cs] = jnp.where(first, 0.0, gpad[0:8, cs])
        gpad[8:8 + tm, cs] = gate
        p0 = sf_ref[0, :, cs]
        p1 = sf_ref[1, :, cs]
        gm1 = jnp.where(is_s, jnp.concatenate([p1, gate[:tm - DEC_BATCH]], axis=0), gpad[7:7 + tm, cs])
        gm2 = jnp.where(is_s, jnp.concatenate([p0, p1, gate[:tm - 2 * DEC_BATCH]], axis=0), gpad[6:6 + tm, cs])
        gc = cb_ref[:, cs] + cw_ref[2:3, cs] * gate + cw_ref[1:2, cs] * gm1 + cw_ref[0:1, cs] * gm2
        act_ref[:, cs] = (_silu(gc) * val).astype(act_ref.dtype)
        gpad[0:8, cs] = gpad[tm:tm + 8, cs]
        nfp_ref[:, cs] = gate[tm - (FFN_CONV - 1):tm]
        nfs_ref[0, :, cs] = gate[(DEC_SEQ - 2) * DEC_BATCH:(DEC_SEQ - 1) * DEC_BATCH]
        nfs_ref[1, :, cs] = gate[(DEC_SEQ - 1) * DEC_BATCH:DEC_SEQ * DEC_BATCH]


def _ffn_up(h2, w_up, conv_w, conv_b, sffn_t, l):
    tm, tn = TS, 512
    npt = TP // tm
    tpb = SEQ // tm
    nn = pl.cdiv(D_FF, tn)
    val0 = D_FF // LANES
    last = 2 * D_FF // LANES - 1

    def vspec(k):
        return pl.BlockSpec((None, D_MODEL, LANES),
                            lambda j, m: (l, 0, jnp.minimum(val0 + (tn // LANES) * j + k, last)))

    return pl.pallas_call(
        functools.partial(_up_kernel, npt=npt, tpb=tpb, tm=tm, tn=tn),
        grid=(nn, T // tm),
        in_specs=[pl.BlockSpec((tm, D_MODEL), lambda j, m: (m, 0)),
                  pl.BlockSpec((None, D_MODEL, tn), lambda j, m: (l, 0, j))]
                 + [vspec(k) for k in range(tn // LANES)]
                 + [pl.BlockSpec((FFN_CONV, tn), lambda j, m: (0, j)),
                    pl.BlockSpec((1, tn), lambda j, m: (0, j)),
                    pl.BlockSpec((None, FFN_CONV - 1, DEC_BATCH, tn), lambda j, m: (l, 0, 0, j))],
        out_specs=[pl.BlockSpec((tm, tn), lambda j, m: (m, j)),
                   pl.BlockSpec((None, FFN_CONV - 1, tn), lambda j, m: (jnp.minimum(m // tpb, BATCH), 0, j)),
                   pl.BlockSpec((FFN_CONV - 1, DEC_BATCH, tn), lambda j, m: (0, 0, j))],
        out_shape=[jax.ShapeDtypeStruct((T, D_FF), BF16),
                   jax.ShapeDtypeStruct((BATCH + 1, FFN_CONV - 1, D_FF), F32),
                   jax.ShapeDtypeStruct((FFN_CONV - 1, DEC_BATCH, D_FF), F32)],
        scratch_shapes=[pltpu.VMEM((D_MODEL, tn), BF16), pltpu.VMEM((D_MODEL, tn), BF16),
                        pltpu.VMEM((8 + tm, tn), F32)],
        compiler_params=_cparams(("arbitrary", "arbitrary")),
        name="ffn_up",
    )(h2, w_up, w_up, w_up, w_up, w_up, conv_w, conv_b.reshape(1, D_FF), sffn_t)


def _pad_lanes(v):
    return jnp.pad(v, ((0, 0), (0, LANES - v.shape[-1])))


def kernel(x_prompt, x_sample, c_prompt, c_sample, state_pool, state_conv, state_ssm, state_ffn, norm1, norm2, w_ada, b_ada, w_in, pool_w, pool_scale, conv_w, conv_b, dt_bias, a_log, d_skip, ssd_norm, w_out, w_up, ffn_conv_w, ffn_conv_b, w_down, norm_f):
    x = jnp.concatenate([x_prompt.reshape(TP, D_MODEL),
                         jnp.swapaxes(x_sample, 0, 1).reshape(TS, D_MODEL)], axis=0)
    c_all = jnp.concatenate([c_sample, c_prompt, jnp.zeros((4, D_MODEL), F32)], axis=0)
    mods, modp = _ada(c_all, w_ada, b_ada)
    modp = modp.reshape(DEPTH, BATCH, 1, 6 * D_MODEL)

    spool_t = jnp.swapaxes(state_pool, 1, 2)
    sconv_t = jnp.swapaxes(state_conv, 1, 2)
    sffn_t = jnp.swapaxes(state_ffn, 1, 2)

    w_out_b = w_out.astype(BF16)
    w_down_b = w_down.astype(BF16)

    d_skip_e = jnp.repeat(d_skip, SSD_HEAD_DIM, axis=-1)

    outs = {k: [] for k in ("pp", "pc", "ps", "pf", "sp", "sc", "sf")}
    h = _norm0(x, norm1[0], modp, mods, 0)
    nssm_stack = None
    for l in range(DEPTH):
        lw = {"pool_w": pool_w[l], "pool_scale": pool_scale[l].reshape(1, POOL_WIDTH),
              "conv_w": conv_w[l], "conv_b": conv_b[l].reshape(1, XBC_WIDTH),
              "dt_bias": _pad_lanes(dt_bias[l].reshape(1, SSD_HEADS)),
              "a_log": _pad_lanes(a_log[l].reshape(1, SSD_HEADS)),
              "d_skip_e": d_skip_e[l].reshape(1, SSD_WIDTH),
              "ssd_norm": ssd_norm[l].reshape(1, SSD_WIDTH)}
        proj, dtp = _inproj(h, w_in, l)
        mix_p, st_p = _mix_prompt(proj, dtp, lw)
        mix_s, nconv_s, nssm_stack = _mix_sample(proj, dtp, sconv_t, spool_t, state_ssm, lw, l, nssm_stack)
        x1, h2 = _resid_proj([mix_p, mix_s.reshape(TS, D_MODEL)], w_out_b, x, norm2[l], modp, mods, l, 2, (l, 3, 4))
        act, nffn_p, nffn_s = _ffn_up(h2, w_up, ffn_conv_w[l], ffn_conv_b[l], sffn_t, l)
        if l + 1 < DEPTH:
            x, h = _resid_proj([act], w_down_b, x1, norm1[l + 1], modp, mods, l, 5, (l + 1, 0, 1))
        else:
            y_p, y_s = _resid_proj([act], w_down_b, x1, norm_f, modp, mods, l, 5, None)

        ends = [(b + 1) * SEQ for b in range(BATCH)]
        outs["pp"].append(jnp.stack([lax.slice(proj, (e - POOL_BUF, 0), (e, POOL_WIDTH)) for e in ends]))
        outs["pc"].append(jnp.stack([lax.slice(proj, (e - (SSD_CONV - 1), POOL_WIDTH + SSD_WIDTH), (e, MAIN_WIDTH))
                                     for e in ends]))
        outs["ps"].append(jnp.transpose(st_p.reshape(BATCH, D_STATE, SSD_HEADS, SSD_HEAD_DIM), (0, 2, 3, 1)))
        outs["pf"].append(nffn_p[:BATCH])
        u_s = jnp.swapaxes(lax.slice(proj, (TP, 0), (T, POOL_WIDTH)).reshape(DEC_SEQ, DEC_BATCH, POOL_WIDTH), 0, 1)
        outs["sp"].append(jnp.concatenate([state_pool[l][:, DEC_SEQ:], u_s], axis=1))
        outs["sc"].append(jnp.swapaxes(nconv_s, 0, 1))
        outs["sf"].append(jnp.swapaxes(nffn_s, 0, 1))

    y_prompt = y_p.reshape(BATCH, SEQ, D_MODEL)
    y_sample = jnp.swapaxes(y_s.reshape(DEC_SEQ, DEC_BATCH, D_MODEL), 0, 1)
    return (y_prompt, y_sample,
            jnp.stack(outs["pp"]), jnp.stack(outs["pc"]), jnp.stack(outs["ps"]), jnp.stack(outs["pf"]),
            jnp.stack(outs["sp"]), jnp.stack(outs["sc"]), nssm_stack, jnp.stack(outs["sf"]))
```

```python
import functools

import jax
import jax.numpy as jnp
from jax import lax
from jax.experimental import pallas as pl
from jax.experimental.pallas import tpu as pltpu

F32 = jnp.float32
BF16 = jnp.bfloat16

D_MODEL = 2048
BATCH = 4
SEQ = 2048
DEPTH = 4
DEC_BATCH = 128
DEC_SEQ = 4
PAST_LEN = 16384

POOL_WIDTH = 512
POOL_WINDOWS = (2, 4, 8, 16)
POOL_GROUP_DIM = 128
POOL_BUF = 15
SSD_WIDTH = 1536
SSD_HEAD_DIM = 64
SSD_HEADS = 24
SSD_GROUPS = 4
SSD_HPG = 6
D_STATE = 128
SSD_CONV = 4
XBC_WIDTH = 2560
MAIN_WIDTH = POOL_WIDTH + SSD_WIDTH + XBC_WIDTH
D_FF = 5504
FFN_CONV = 3
EPS = 1e-6

TP = BATCH * SEQ
TS = DEC_SEQ * DEC_BATCH
T = TP + TS
GROUP_COLS = SSD_HPG * SSD_HEAD_DIM
LANES = 128
CHUNK = 128

VMEM_LIMIT = 56 * 1024 * 1024


def _silu(x):
    return x * jax.nn.sigmoid(x)


def _softplus(x):
    return jnp.maximum(x, 0.0) + jnp.log1p(jnp.exp(-jnp.abs(x)))


def _cparams(sem):
    return pltpu.CompilerParams(dimension_semantics=sem, vmem_limit_bytes=VMEM_LIMIT)


def _rowmod(p_ref, s_ref, is_sample, tm):
    s = s_ref[...]
    if tm > DEC_BATCH:
        s = jnp.concatenate([s] * (tm // DEC_BATCH), axis=0)
    return jnp.where(is_sample, s, p_ref[...])


def _rms_mod(x, gain, sc, sh):
    ms = jnp.mean(x * x, axis=-1, keepdims=True)
    y = x * lax.rsqrt(ms + EPS) * gain
    return y * (1.0 + sc) + sh


def _ada_kernel(c_ref, w_ref, b_ref, os_ref, op_ref):
    a = _silu(c_ref[...]).astype(BF16)
    m = jnp.dot(a, w_ref[...].astype(BF16), preferred_element_type=F32) + b_ref[...]
    os_ref[...] = m[:DEC_BATCH]
    op_ref[...] = m[DEC_BATCH:DEC_BATCH + BATCH]


def _ada(c_all, w_ada, b_ada):
    tn = 1024
    rows = c_all.shape[0]
    return pl.pallas_call(
        _ada_kernel,
        grid=(DEPTH, 6 * D_MODEL // tn),
        in_specs=[
            pl.BlockSpec((rows, D_MODEL), lambda l, j: (0, 0)),
            pl.BlockSpec((None, D_MODEL, tn), lambda l, j: (l, 0, j)),
            pl.BlockSpec((None, 1, tn), lambda l, j: (l, 0, j)),
        ],
        out_specs=[
            pl.BlockSpec((None, DEC_BATCH, tn), lambda l, j: (l, 0, j)),
            pl.BlockSpec((None, BATCH, tn), lambda l, j: (l, 0, j)),
        ],
        out_shape=[
            jax.ShapeDtypeStruct((DEPTH, DEC_BATCH, 6 * D_MODEL), F32),
            jax.ShapeDtypeStruct((DEPTH, BATCH, 6 * D_MODEL), F32),
        ],
        compiler_params=_cparams(("arbitrary", "arbitrary")),
        name="ada",
    )(c_all, w_ada, b_ada.reshape(DEPTH, 1, 6 * D_MODEL))


def _mod_specs(l, k, tm, row_of):
    tpb = SEQ // tm

    def pmap(*ids):
        return (l, jnp.minimum(row_of(*ids) // tpb, BATCH - 1), 0, k)

    def smap(*ids):
        return (l, 0, k)

    return (pl.BlockSpec((None, None, 1, D_MODEL), pmap), pl.BlockSpec((None, DEC_BATCH, D_MODEL), smap))


def _norm_kernel(x_ref, g_ref, scp, scs, shp, shs, o_ref, *, npt, tm):
    is_s = pl.program_id(0) >= npt
    sc = _rowmod(scp, scs, is_s, tm)
    sh = _rowmod(shp, shs, is_s, tm)
    o_ref[...] = _rms_mod(x_ref[...], g_ref[...], sc, sh).astype(o_ref.dtype)


def _norm0(x, gain, modp, mods, l):
    tm = 256
    npt = TP // tm
    row = lambda i: i
    scp, scs = _mod_specs(l, 1, tm, row)
    shp, shs = _mod_specs(l, 0, tm, row)
    return pl.pallas_call(
        functools.partial(_norm_kernel, npt=npt, tm=tm),
        grid=(T // tm,),
        in_specs=[pl.BlockSpec((tm, D_MODEL), lambda i: (i, 0)),
                  pl.BlockSpec((1, D_MODEL), lambda i: (0, 0)),
                  scp, scs, shp, shs],
        out_specs=pl.BlockSpec((tm, D_MODEL), lambda i: (i, 0)),
        out_shape=jax.ShapeDtypeStruct((T, D_MODEL), BF16),
        compiler_params=_cparams(("arbitrary",)),
        name="norm0",
    )(x, gain.reshape(1, D_MODEL), modp, mods, modp, mods)


_NT = (((1,), (1,)), ((), ()))


def _inproj_kernel(h_ref, w_ref, wdt_ref, o_ref, dt_ref, wb, wdtb):
    j = pl.program_id(0)
    m = pl.program_id(1)

    @pl.when(m == 0)
    def _():
        wb[...] = w_ref[...].astype(BF16)

    h = h_ref[...]
    o_ref[...] = lax.dot_general(h, wb[...], _NT, preferred_element_type=F32)

    @pl.when(j == 0)
    def _():
        @pl.when(m == 0)
        def _():
            row = lax.broadcasted_iota(jnp.int32, (LANES, 1), 0)
            wdtb[...] = jnp.where(row < SSD_HEADS, wdt_ref[...], 0.0).astype(BF16)

        dt_ref[...] = lax.dot_general(h, wdtb[...], _NT, preferred_element_type=F32)

    @pl.when(j != 0)
    def _():
        dt_ref[...] = jnp.zeros(dt_ref.shape, F32)


def _inproj(h, w_in_t, l):
    tm, tn = 512, 1536
    nm = T // tm
    return pl.pallas_call(
        _inproj_kernel,
        grid=(MAIN_WIDTH // tn, nm),
        in_specs=[pl.BlockSpec((tm, D_MODEL), lambda j, m: (m, 0)),
                  pl.BlockSpec((None, tn, D_MODEL), lambda j, m: (l, j, 0)),
                  pl.BlockSpec((None, LANES, D_MODEL), lambda j, m: (l, MAIN_WIDTH // LANES, 0))],
        out_specs=[pl.BlockSpec((tm, tn), lambda j, m: (m, j)),
                   pl.BlockSpec((tm, LANES), lambda j, m: (jnp.where(j == 0, m, nm), 0))],
        out_shape=[jax.ShapeDtypeStruct((T, MAIN_WIDTH), F32),
                   jax.ShapeDtypeStruct((T + tm, LANES), F32)],
        scratch_shapes=[pltpu.VMEM((tn, D_MODEL), BF16), pltpu.VMEM((LANES, D_MODEL), BF16)],
        compiler_params=_cparams(("arbitrary", "arbitrary")),
        name="inproj",
    )(h, w_in_t, w_in_t)


def _expand_heads(v, rows):
    lo = lax.broadcasted_iota(jnp.int32, (rows, LANES), 1) < SSD_HEAD_DIM
    parts = []
    for j in range(SSD_HEADS // 2):
        a = jnp.broadcast_to(v[:, 2 * j:2 * j + 1], (rows, LANES))
        b = jnp.broadcast_to(v[:, 2 * j + 1:2 * j + 2], (rows, LANES))
        parts.append(jnp.where(lo, a, b))
    return jnp.concatenate(parts, axis=1)


def _split3(x):
    a = x.astype(BF16)
    r = x - a.astype(F32)
    b = r.astype(BF16)
    c = (r - b.astype(F32)).astype(BF16)
    return a, b, c


def _gated_norm(y, z, gain):
    y = y * _silu(z)
    ms = jnp.mean(y * y, axis=-1, keepdims=True)
    return y * lax.rsqrt(ms + EPS) * gain


def _mixp_kernel(proj_ref, dt_ref, poolw_ref, pscale_ref, cw_ref, cb_ref, dtb_ref, alog_ref, dskip_ref, ng_ref,
                 mix_ref, st_ref, xpad, upad, hT):
    c = pl.program_id(1)
    Q = CHUNK

    @pl.when(c == 0)
    def _():
        xpad[0:8, :] = jnp.zeros((8, XBC_WIDTH), F32)
        upad[0:16, :] = jnp.zeros((16, POOL_WIDTH), F32)
        hT[...] = jnp.zeros(hT.shape, F32)

    u = proj_ref[:, 0:POOL_WIDTH]
    upad[16:16 + Q, :] = u
    pos = c * Q + lax.broadcasted_iota(jnp.int32, (Q, 1), 0)
    outs = []
    for g, w in enumerate(POOL_WINDOWS):
        lo = g * POOL_GROUP_DIM
        ug = u[:, lo:lo + POOL_GROUP_DIM]
        s = ug
        for k in range(1, w):
            s = s + upad[16 - k:16 - k + Q, lo:lo + POOL_GROUP_DIM]
        cnt = jnp.minimum(pos + 1, w).astype(F32)
        d = s / cnt - ug
        outs.append(jnp.dot(d.astype(BF16), poolw_ref[g].astype(BF16), preferred_element_type=F32))
    mix_ref[:, 0:POOL_WIDTH] = (jnp.concatenate(outs, axis=1) * pscale_ref[...]).astype(mix_ref.dtype)
    upad[0:16, :] = upad[Q:Q + 16, :]

    xbc = proj_ref[:, POOL_WIDTH + SSD_WIDTH:MAIN_WIDTH]
    xpad[8:8 + Q, :] = xbc
    acc = cb_ref[...] + cw_ref[3:4, :] * xbc
    for i in range(SSD_CONV - 1):
        acc = acc + cw_ref[i:i + 1, :] * xpad[5 + i:5 + i + Q, :]
    xpad[0:8, :] = xpad[Q:Q + 8, :]
    xc = _silu(acc)
    xs = xc[:, 0:SSD_WIDTH]
    Bb = xc[:, SSD_WIDTH:SSD_WIDTH + SSD_GROUPS * D_STATE].astype(BF16)
    Cb = xc[:, SSD_WIDTH + SSD_GROUPS * D_STATE:XBC_WIDTH].astype(BF16)

    dt = _softplus(dt_ref[...] + dtb_ref[...])
    dA = dt * (-jnp.exp(alog_ref[...]))
    row = lax.broadcasted_iota(jnp.int32, (Q, Q), 0)
    col = lax.broadcasted_iota(jnp.int32, (Q, Q), 1)
    tri = row >= col
    tri_lo = jnp.where(tri, 1.0, 0.0).astype(BF16)
    tri_up = jnp.where(row <= col, 1.0, 0.0).astype(BF16)
    pieces = _split3(dA)
    cs = sum(jnp.dot(tri_lo, p, preferred_element_type=F32) for p in pieces)
    csT = sum(lax.dot_general(p, tri_up, (((0,), (0,)), ((), ())), preferred_element_type=F32)
              for p in pieces)
    cs_last = cs[Q - 1:Q, :]
    expcs_e = _expand_heads(jnp.exp(cs), Q)
    dte_e = _expand_heads(jnp.exp(cs_last - cs), Q)
    xdt = xs * _expand_heads(dt, Q)
    xdt_b = xdt.astype(BF16)
    lo_half = lax.broadcasted_iota(jnp.int32, (Q, LANES), 1) < SSD_HEAD_DIM
    xw_b = (xdt * dte_e).astype(BF16)
    decay_e = expcs_e[Q - 1:Q, :]
    h_prev = hT[...]
    h_prev_b = h_prev.astype(BF16)

    ys = []
    for g in range(SSD_GROUPS):
        gs = slice(g * GROUP_COLS, (g + 1) * GROUP_COLS)
        Cg = Cb[:, g * D_STATE:(g + 1) * D_STATE]
        Bg = Bb[:, g * D_STATE:(g + 1) * D_STATE]
        CB = lax.dot_general(Cg, Bg, (((1,), (1,)), ((), ())), preferred_element_type=F32)
        y_off = jnp.dot(Cg, h_prev_b[:, gs], preferred_element_type=F32)
        ST = lax.dot_general(Bg, xw_b[:, gs], (((0,), (0,)), ((), ())), preferred_element_type=F32)
        hT[:, gs] = h_prev[:, gs] * decay_e[:, gs] + ST
        parts = []
        for j in range(SSD_HPG // 2):
            h1 = SSD_HPG * g + 2 * j
            ps = slice(h1 * SSD_HEAD_DIM, h1 * SSD_HEAD_DIM + LANES)
            W = []
            for h in (h1, h1 + 1):
                seg = cs[:, h:h + 1] - csT[h:h + 1, :]
                W.append((CB * jnp.exp(jnp.where(tri, seg, -jnp.inf))).astype(BF16))
            parts.append(jnp.where(lo_half,
                                   jnp.dot(W[0], xdt_b[:, ps], preferred_element_type=F32),
                                   jnp.dot(W[1], xdt_b[:, ps], preferred_element_type=F32)))
        ys.append(jnp.concatenate(parts, axis=1) + y_off * expcs_e[:, gs])
    y = jnp.concatenate(ys, axis=1) + dskip_ref[...] * xs
    z = proj_ref[:, POOL_WIDTH:POOL_WIDTH + SSD_WIDTH]
    mix_ref[:, POOL_WIDTH:D_MODEL] = _gated_norm(y, z, ng_ref[...]).astype(mix_ref.dtype)

    @pl.when(c == pl.num_programs(1) - 1)
    def _():
        st_ref[...] = hT[...]


def _mix_prompt(proj, dtp, lw):
    nc = SEQ // CHUNK
    full = lambda shape: pl.BlockSpec(shape, lambda b, c: (0,) * len(shape))
    return pl.pallas_call(
        _mixp_kernel,
        grid=(BATCH, nc),
        in_specs=[pl.BlockSpec((CHUNK, MAIN_WIDTH), lambda b, c: (b * nc + c, 0)),
                  pl.BlockSpec((CHUNK, LANES), lambda b, c: (b * nc + c, 0)),
                  full((len(POOL_WINDOWS), POOL_GROUP_DIM, POOL_GROUP_DIM)),
                  full((1, POOL_WIDTH)),
                  full((SSD_CONV, XBC_WIDTH)),
                  full((1, XBC_WIDTH)),
                  full((1, LANES)), full((1, LANES)),
                  full((1, SSD_WIDTH)), full((1, SSD_WIDTH))],
        out_specs=[pl.BlockSpec((CHUNK, D_MODEL), lambda b, c: (b * nc + c, 0)),
                   pl.BlockSpec((None, D_STATE, SSD_WIDTH), lambda b, c: (b, 0, 0))],
        out_shape=[jax.ShapeDtypeStruct((TP, D_MODEL), BF16),
                   jax.ShapeDtypeStruct((BATCH, D_STATE, SSD_WIDTH), F32)],
        scratch_shapes=[pltpu.VMEM((8 + CHUNK, XBC_WIDTH), F32),
                        pltpu.VMEM((16 + CHUNK, POOL_WIDTH), F32),
                        pltpu.VMEM((D_STATE, SSD_WIDTH), F32)],
        compiler_params=_cparams(("arbitrary", "arbitrary")),
        name="mix_prompt",
    )(proj, dtp, lw["pool_w"], lw["pool_scale"], lw["conv_w"], lw["conv_b"], lw["dt_bias"], lw["a_log"],
      lw["d_skip_e"], lw["ssd_norm"])


SAMPLE_BT = 8


def _mixs_kernel(*refs, aliased):
    if aliased:
        refs = refs[1:]
    (p0, p1, p2, p3, d0, d1, d2, d3, sconv_ref, spool_ref, sssm_ref,
     poolw_ref, pscale_ref, cw_ref, cb_ref, dtb_ref, alog_ref, dskip_ref, ng_ref,
     mix_ref, nconv_ref, nssm_ref, c_scr, b_scr, xw_scr, yoff_scr) = refs
    bt = SAMPLE_BT
    P = (p0, p1, p2, p3)
    Dt = (d0, d1, d2, d3)
    L = DEC_SEQ

    fullp = [spool_ref[k] for k in range(POOL_BUF)] + [P[t][:, 0:POOL_WIDTH] for t in range(L)]
    for g, w in enumerate(POOL_WINDOWS):
        gl = slice(g * POOL_GROUP_DIM, (g + 1) * POOL_GROUP_DIM)
        ds = []
        for t in range(L):
            s = fullp[POOL_BUF + t][:, gl]
            for k in range(1, w):
                s = s + fullp[POOL_BUF + t - k][:, gl]
            cnt = float(min(PAST_LEN + t + 1, w))
            ds.append(s / cnt - fullp[POOL_BUF + t][:, gl])
        o = jnp.dot(jnp.concatenate(ds, axis=0).astype(BF16), poolw_ref[g].astype(BF16),
                    preferred_element_type=F32) * pscale_ref[:, gl]
        for t in range(L):
            mix_ref[t, :, gl] = o[t * bt:(t + 1) * bt]

    fullc = [sconv_ref[k] for k in range(SSD_CONV - 1)] + [P[t][:, POOL_WIDTH + SSD_WIDTH:MAIN_WIDTH] for t in range(L)]
    for k in range(SSD_CONV - 1):
        nconv_ref[k] = fullc[L + k]
    xs, Bm, Cm = [], [], []
    for t in range(L):
        acc = cb_ref[...]
        for i in range(SSD_CONV):
            acc = acc + cw_ref[i:i + 1, :] * fullc[t + i]
        xc = _silu(acc)
        xs.append(xc[:, 0:SSD_WIDTH])
        Bm.append(xc[:, SSD_WIDTH:SSD_WIDTH + SSD_GROUPS * D_STATE])
        Cm.append(xc[:, SSD_WIDTH + SSD_GROUPS * D_STATE:XBC_WIDTH])

    A = -jnp.exp(alog_ref[...])
    dts = [_softplus(Dt[t][...] + dtb_ref[...]) for t in range(L)]
    cs = []
    for t in range(L):
        dA = dts[t] * A
        cs.append(dA if t == 0 else cs[t - 1] + dA)
    dec24 = jnp.exp(cs[L - 1])
    expcs_e = [_expand_heads(jnp.exp(cs[t]), bt) for t in range(L)]
    xdt = [xs[t] * _expand_heads(dts[t], bt) for t in range(L)]

    ydiag = []
    for t in range(L):
        acc = None
        for s in range(t + 1):
            cb_parts = []
            for g in range(SSD_GROUPS):
                sl = slice(g * D_STATE, (g + 1) * D_STATE)
                r = jnp.sum(Cm[t][:, sl] * Bm[s][:, sl], axis=-1, keepdims=True)
                cb_parts.append(jnp.broadcast_to(r, (bt, GROUP_COLS)))
            coef = jnp.concatenate(cb_parts, axis=1)
            if s < t:
                coef = coef * _expand_heads(jnp.exp(cs[t] - cs[s]), bt)
            term = coef * xdt[s]
            acc = term if acc is None else acc + term
        ydiag.append(acc)

    for t in range(L):
        c_scr[t] = Cm[t]
        b_scr[t] = Bm[t]
        xw_scr[t] = xdt[t] * _expand_heads(jnp.exp(cs[L - 1] - cs[t]), bt)
    zpad_s = jnp.zeros((8 - L, D_STATE), F32)
    zpad_x = jnp.zeros((8 - L, GROUP_COLS), F32)
    for b in range(bt):
        for g in range(SSD_GROUPS):
            sl = slice(g * D_STATE, (g + 1) * D_STATE)
            gs = slice(g * GROUP_COLS, (g + 1) * GROUP_COLS)
            C8 = jnp.concatenate([c_scr[t, b:b + 1, sl] for t in range(L)] + [zpad_s], axis=0)
            B8 = jnp.concatenate([b_scr[t, b:b + 1, sl] for t in range(L)] + [zpad_s], axis=0)
            X8 = jnp.concatenate([xw_scr[t, b:b + 1, gs] for t in range(L)] + [zpad_x], axis=0)
            h0 = sssm_ref[b, SSD_HPG * g:SSD_HPG * (g + 1)].reshape(GROUP_COLS, D_STATE)
            GT = lax.dot_general(C8.astype(BF16), h0.astype(BF16), (((1,), (1,)), ((), ())),
                                 preferred_element_type=F32)
            U = lax.dot_general(X8, B8, (((0,), (0,)), ((), ())), preferred_element_type=F32)
            for r in range(SSD_HPG):
                h = SSD_HPG * g + r
                dec = jnp.broadcast_to(dec24[b:b + 1, h:h + 1], (SSD_HEAD_DIM, D_STATE))
                rs = slice(r * SSD_HEAD_DIM, (r + 1) * SSD_HEAD_DIM)
                nssm_ref[b, h] = h0[rs] * dec + U[rs]
            for t in range(L):
                yoff_scr[t, b:b + 1, gs] = GT[t:t + 1, :]

    for t in range(L):
        y = ydiag[t] + expcs_e[t] * yoff_scr[t] + dskip_ref[...] * xs[t]
        z = P[t][:, POOL_WIDTH:POOL_WIDTH + SSD_WIDTH]
        mix_ref[t, :, POOL_WIDTH:D_MODEL] = _gated_norm(y, z, ng_ref[...])


def _mix_sample(proj, dtp, sconv_t, spool_t, sssm, lw, l, nssm_stack):
    bt = SAMPLE_BT
    nb = DEC_BATCH // bt
    base = TP // bt
    aliased = nssm_stack is not None

    def pspec(t, width):
        return pl.BlockSpec((bt, width), lambda i, t=t: (base + t * nb + i, 0))

    full = lambda shape: pl.BlockSpec(shape, lambda i: (0,) * len(shape))
    return pl.pallas_call(
        functools.partial(_mixs_kernel, aliased=aliased),
        grid=(nb,),
        input_output_aliases={0: 2} if aliased else {},
        in_specs=([pl.BlockSpec(memory_space=pl.ANY)] if aliased else [])
        + [pspec(t, MAIN_WIDTH) for t in range(DEC_SEQ)] + [pspec(t, LANES) for t in range(DEC_SEQ)] + [
            pl.BlockSpec((None, SSD_CONV - 1, bt, XBC_WIDTH), lambda i: (l, 0, i, 0)),
            pl.BlockSpec((None, POOL_BUF, bt, POOL_WIDTH), lambda i: (l, 0, i, 0)),
            pl.BlockSpec((None, bt, SSD_HEADS, SSD_HEAD_DIM, D_STATE), lambda i: (l, i, 0, 0, 0)),
            full((len(POOL_WINDOWS), POOL_GROUP_DIM, POOL_GROUP_DIM)),
            full((1, POOL_WIDTH)),
            full((SSD_CONV, XBC_WIDTH)),
            full((1, XBC_WIDTH)),
            full((1, LANES)), full((1, LANES)),
            full((1, SSD_WIDTH)), full((1, SSD_WIDTH))],
        out_specs=[pl.BlockSpec((DEC_SEQ, bt, D_MODEL), lambda i: (0, i, 0)),
                   pl.BlockSpec((SSD_CONV - 1, bt, XBC_WIDTH), lambda i: (0, i, 0)),
                   pl.BlockSpec((None, bt, SSD_HEADS, SSD_HEAD_DIM, D_STATE), lambda i: (l, i, 0, 0, 0))],
        out_shape=[jax.ShapeDtypeStruct((DEC_SEQ, DEC_BATCH, D_MODEL), F32),
                   jax.ShapeDtypeStruct((SSD_CONV - 1, DEC_BATCH, XBC_WIDTH), F32),
                   jax.ShapeDtypeStruct((DEPTH, DEC_BATCH, SSD_HEADS, SSD_HEAD_DIM, D_STATE), F32)],
        scratch_shapes=[pltpu.VMEM((DEC_SEQ, bt, SSD_GROUPS * D_STATE), F32),
                        pltpu.VMEM((DEC_SEQ, bt, SSD_GROUPS * D_STATE), F32),
                        pltpu.VMEM((DEC_SEQ, bt, SSD_WIDTH), F32),
                        pltpu.VMEM((DEC_SEQ, bt, SSD_WIDTH), F32)],
        compiler_params=_cparams(("arbitrary",)),
        name="mix_sample",
    )(*([nssm_stack] if aliased else []),
      proj, proj, proj, proj, dtp, dtp, dtp, dtp, sconv_t, spool_t, sssm,
      lw["pool_w"], lw["pool_scale"], lw["conv_w"], lw["conv_b"], lw["dt_bias"], lw["a_log"],
      lw["d_skip_e"], lw["ssd_norm"])


def _resid_epilogue(acc, x_ref, gp, gs, ng_ref, mods, is_s, tm):
    xn = x_ref[...] + _rowmod(gp, gs, is_s, tm) * acc
    if mods is None:
        ms = jnp.mean(xn * xn, axis=-1, keepdims=True)
        return xn, xn * lax.rsqrt(ms + EPS) * ng_ref[...]
    scp, scs, shp, shs = mods
    return xn, _rms_mod(xn, ng_ref[...], _rowmod(scp, scs, is_s, tm), _rowmod(shp, shs, is_s, tm))


def _outproj_kernel(ap_ref, as_ref, w_ref, x_ref, gp, gs, ng_ref, scp, scs, shp, shs, xo_ref, ho_ref, *, npt, tm):
    is_s = pl.program_id(0) >= npt
    a = jnp.where(is_s, as_ref[...].astype(BF16), ap_ref[...])
    acc = jnp.dot(a, w_ref[...], preferred_element_type=F32)
    xn, hn = _resid_epilogue(acc, x_ref, gp, gs, ng_ref, (scp, scs, shp, shs), is_s, tm)
    xo_ref[...] = xn
    ho_ref[...] = hn.astype(ho_ref.dtype)


def _down_kernel(a_ref, w_ref, x_ref, gp, gs, ng_ref, scp, scs, shp, shs, xo_ref, ho_ref, *, npt, tm):
    is_s = pl.program_id(0) >= npt
    acc = jnp.dot(a_ref[...], w_ref[...], preferred_element_type=F32)
    xn, hn = _resid_epilogue(acc, x_ref, gp, gs, ng_ref, (scp, scs, shp, shs), is_s, tm)
    xo_ref[...] = xn
    ho_ref[...] = hn.astype(ho_ref.dtype)


def _final_kernel(a_ref, w_ref, x_ref, gp, gs, ng_ref, yp_ref, ys_ref, *, npt, tm):
    is_s = pl.program_id(0) >= npt
    acc = jnp.dot(a_ref[...], w_ref[...], preferred_element_type=F32)
    _, yn = _resid_epilogue(acc, x_ref, gp, gs, ng_ref, None, is_s, tm)

    @pl.when(jnp.logical_not(is_s))
    def _():
        yp_ref[...] = yn

    @pl.when(is_s)
    def _():
        ys_ref[...] = yn


def _resid_proj(a_list, w, x, gain, modp, mods, l, gate_k, next_mod):
    tm = 256
    npt = TP // tm
    row = lambda i: i
    kdim = w.shape[-2]
    if w.ndim == 3:
        w_spec = pl.BlockSpec((None, kdim, D_MODEL), lambda i: (l, 0, 0), pipeline_mode=pl.Buffered(1))
    else:
        w_spec = pl.BlockSpec((kdim, D_MODEL), lambda i: (0, 0), pipeline_mode=pl.Buffered(1))
    gp, gs = _mod_specs(l, gate_k, tm, row)
    if len(a_list) == 2:
        in_specs = [pl.BlockSpec((tm, kdim), lambda i: (jnp.minimum(i, npt - 1), 0)),
                    pl.BlockSpec((tm, kdim), lambda i: (jnp.maximum(i - npt, 0), 0))]
        kern = _outproj_kernel
    else:
        in_specs = [pl.BlockSpec((tm, kdim), lambda i: (i, 0))]
        kern = _down_kernel
    in_specs += [w_spec,
                 pl.BlockSpec((tm, D_MODEL), lambda i: (i, 0)),
                 gp, gs,
                 pl.BlockSpec((1, D_MODEL), lambda i: (0, 0))]
    args = list(a_list) + [w, x, modp, mods, gain.reshape(1, D_MODEL)]
    if next_mod is None:
        kern = _final_kernel
        out_specs = [pl.BlockSpec((tm, D_MODEL), lambda i: (jnp.minimum(i, npt - 1), 0)),
                     pl.BlockSpec((tm, D_MODEL), lambda i: (jnp.maximum(i - npt, 0), 0))]
        out_shape = [jax.ShapeDtypeStruct((TP, D_MODEL), F32), jax.ShapeDtypeStruct((TS, D_MODEL), F32)]
    else:
        nl, shift_k, scale_k = next_mod
        scp, scs = _mod_specs(nl, scale_k, tm, row)
        shp, shs = _mod_specs(nl, shift_k, tm, row)
        in_specs += [scp, scs, shp, shs]
        args += [modp, mods, modp, mods]
        out_specs = [pl.BlockSpec((tm, D_MODEL), lambda i: (i, 0)),
                     pl.BlockSpec((tm, D_MODEL), lambda i: (i, 0))]
        out_shape = [jax.ShapeDtypeStruct((T, D_MODEL), F32), jax.ShapeDtypeStruct((T, D_MODEL), BF16)]
    return pl.pallas_call(
        functools.partial(kern, npt=npt, tm=tm),
        grid=(T // tm,),
        in_specs=in_specs,
        out_specs=out_specs,
        out_shape=out_shape,
        compiler_params=_cparams(("arbitrary",)),
        name="resid_proj",
    )(*args)


def _up_kernel(h_ref, wg_ref, wv0, wv1, wv2, wv3, cw_ref, cb_ref, sf_ref, wd_ref,
               act_ref, nfp_ref, nfs_ref, wdb_ref, wgb, wvb, gpad, *, npt, tpb, tm, tn):
    m = pl.program_id(1)

    @pl.when(m == 0)
    def _():
        wgb[...] = wg_ref[...].astype(BF16)
        for k, r in enumerate((wv0, wv1, wv2, wv3)):
            wvb[:, k * LANES:(k + 1) * LANES] = r[...].astype(BF16)
        wdb_ref[...] = wd_ref[...].astype(BF16)

    is_s = m >= npt
    first = (m % tpb) == 0
    hw = tn // 2
    B1, B2 = DEC_BATCH, 2 * DEC_BATCH
    plan = [(0, tm // 2, 0), (tm // 2, tm // 2, 0),
            (0, tm // 2, 1), (tm // 2, tm // 4, 1), (3 * tm // 4, tm // 4, 1)]
    h_halves = (h_ref[0:tm // 2, :], h_ref[tm // 2:tm, :])
    for r0, rows, half in plan:
        cs = slice(half * hw, (half + 1) * hw)
        hc = h_halves[r0 // (tm // 2)]
        if rows < tm // 2:
            q0 = r0 % (tm // 2)
            hc = hc[q0:q0 + rows]
        gate = jnp.dot(hc, wgb[:, cs], preferred_element_type=F32)
        val = jnp.dot(hc, wvb[:, cs], preferred_element_type=F32)
        if r0 == 0:
            gpad[0:8, cs] = jnp.where(first, 0.0, gpad[0:8, cs])
        gpad[8 + r0:8 + r0 + rows, cs] = gate
        gm1 = gpad[7 + r0:7 + r0 + rows, cs]
        gm2 = gpad[6 + r0:6 + r0 + rows, cs]
        if r0 < TS:
            if r0 == 0:
                s1 = jnp.concatenate([sf_ref[1, :, cs], gpad[8:8 + rows - B1, cs]], axis=0)
                s2 = jnp.concatenate([sf_ref[0, :, cs], sf_ref[1, :, cs], gpad[8:8 + rows - B2, cs]], axis=0)
            else:
                s1 = gpad[8 + r0 - B1:8 + r0 - B1 + rows, cs]
                s2 = gpad[8 + r0 - B2:8 + r0 - B2 + rows, cs]
            gm1 = jnp.where(is_s, s1, gm1)
            gm2 = jnp.where(is_s, s2, gm2)
        gc = cb_ref[:, cs] + cw_ref[2:3, cs] * gate + cw_ref[1:2, cs] * gm1 + cw_ref[0:1, cs] * gm2
        act_ref[r0:r0 + rows, cs] = (_silu(gc) * val).astype(act_ref.dtype)
        if r0 + rows == tm:
            gpad[0:8, cs] = gpad[tm:tm + 8, cs]
            nfp_ref[:, cs] = gate[rows - (FFN_CONV - 1):rows]
        if r0 == 0:
            nfs_ref[0, :, cs] = gate[(DEC_SEQ - 2) * B1:(DEC_SEQ - 1) * B1]
            nfs_ref[1, :, cs] = gate[(DEC_SEQ - 1) * B1:DEC_SEQ * B1]


def _ffn_up(h2, w_up, conv_w, conv_b, sffn_t, w_down, l):
    tm, tn = 1024, 512
    assert TS % (tm // 4) == 0 and TS <= tm // 2 and SEQ % tm == 0
    npt = TP // tm
    tpb = SEQ // tm
    nn = pl.cdiv(D_FF, tn)
    val0 = D_FF // LANES
    last = 2 * D_FF // LANES - 1

    def vspec(k):
        return pl.BlockSpec((None, D_MODEL, LANES),
                            lambda j, m: (l, 0, jnp.minimum(val0 + (tn // LANES) * j + k, last)))

    return pl.pallas_call(
        functools.partial(_up_kernel, npt=npt, tpb=tpb, tm=tm, tn=tn),
        grid=(nn, pl.cdiv(T, tm)),
        in_specs=[pl.BlockSpec((tm, D_MODEL), lambda j, m: (m, 0)),
                  pl.BlockSpec((None, D_MODEL, tn), lambda j, m: (l, 0, j))]
                 + [vspec(k) for k in range(tn // LANES)]
                 + [pl.BlockSpec((FFN_CONV, tn), lambda j, m: (0, j)),
                    pl.BlockSpec((1, tn), lambda j, m: (0, j)),
                    pl.BlockSpec((None, FFN_CONV - 1, DEC_BATCH, tn), lambda j, m: (l, 0, 0, j)),
                    pl.BlockSpec((None, tn, D_MODEL), lambda j, m: (l, j, 0))],
        out_specs=[pl.BlockSpec((tm, tn), lambda j, m: (m, j)),
                   pl.BlockSpec((None, FFN_CONV - 1, tn), lambda j, m: (jnp.minimum(m // tpb, BATCH), 0, j)),
                   pl.BlockSpec((FFN_CONV - 1, DEC_BATCH, tn), lambda j, m: (0, 0, j)),
                   pl.BlockSpec((tn, D_MODEL), lambda j, m: (j, 0))],
        out_shape=[jax.ShapeDtypeStruct((T, D_FF), BF16),
                   jax.ShapeDtypeStruct((BATCH + 1, FFN_CONV - 1, D_FF), F32),
                   jax.ShapeDtypeStruct((FFN_CONV - 1, DEC_BATCH, D_FF), F32),
                   jax.ShapeDtypeStruct((D_FF, D_MODEL), BF16)],
        scratch_shapes=[pltpu.VMEM((D_MODEL, tn), BF16), pltpu.VMEM((D_MODEL, tn), BF16),
                        pltpu.VMEM((8 + tm, tn), F32)],
        compiler_params=_cparams(("arbitrary", "arbitrary")),
        name="ffn_up",
    )(h2, w_up, w_up, w_up, w_up, w_up, conv_w, conv_b.reshape(1, D_FF), sffn_t, w_down)


def _pad_lanes(v):
    return jnp.pad(v, ((0, 0), (0, LANES - v.shape[-1])))


def kernel(x_prompt, x_sample, c_prompt, c_sample, state_pool, state_conv, state_ssm, state_ffn, norm1, norm2, w_ada, b_ada, w_in, pool_w, pool_scale, conv_w, conv_b, dt_bias, a_log, d_skip, ssd_norm, w_out, w_up, ffn_conv_w, ffn_conv_b, w_down, norm_f):
    x = jnp.concatenate([x_prompt.reshape(TP, D_MODEL),
                         jnp.swapaxes(x_sample, 0, 1).reshape(TS, D_MODEL)], axis=0)
    c_all = jnp.concatenate([c_sample, c_prompt, jnp.zeros((4, D_MODEL), F32)], axis=0)
    mods, modp = _ada(c_all, w_ada, b_ada)
    modp = modp.reshape(DEPTH, BATCH, 1, 6 * D_MODEL)

    spool_t = jnp.swapaxes(state_pool, 1, 2)
    sconv_t = jnp.swapaxes(state_conv, 1, 2)
    sffn_t = jnp.swapaxes(state_ffn, 1, 2)

    w_in_t = jnp.swapaxes(w_in, 1, 2)
    w_out_b = w_out.astype(BF16)

    d_skip_e = jnp.repeat(d_skip, SSD_HEAD_DIM, axis=-1)

    outs = {k: [] for k in ("pp", "pc", "ps", "pf", "sp", "sc", "sf")}
    h = _norm0(x, norm1[0], modp, mods, 0)
    nssm_stack = None
    for l in range(DEPTH):
        lw = {"pool_w": pool_w[l], "pool_scale": pool_scale[l].reshape(1, POOL_WIDTH),
              "conv_w": conv_w[l], "conv_b": conv_b[l].reshape(1, XBC_WIDTH),
              "dt_bias": _pad_lanes(dt_bias[l].reshape(1, SSD_HEADS)),
              "a_log": _pad_lanes(a_log[l].reshape(1, SSD_HEADS)),
              "d_skip_e": d_skip_e[l].reshape(1, SSD_WIDTH),
              "ssd_norm": ssd_norm[l].reshape(1, SSD_WIDTH)}
        proj, dtp = _inproj(h, w_in_t, l)
        mix_p, st_p = _mix_prompt(proj, dtp, lw)
        mix_s, nconv_s, nssm_stack = _mix_sample(proj, dtp, sconv_t, spool_t, state_ssm, lw, l, nssm_stack)
        x1, h2 = _resid_proj([mix_p, mix_s.reshape(TS, D_MODEL)], w_out_b, x, norm2[l], modp, mods, l, 2, (l, 3, 4))
        act, nffn_p, nffn_s, w_down_b = _ffn_up(h2, w_up, ffn_conv_w[l], ffn_conv_b[l], sffn_t, w_down, l)
        if l + 1 < DEPTH:
            x, h = _resid_proj([act], w_down_b, x1, norm1[l + 1], modp, mods, l, 5, (l + 1, 0, 1))
        else:
            y_p, y_s = _resid_proj([act], w_down_b, x1, norm_f, modp, mods, l, 5, None)

        ends = [(b + 1) * SEQ for b in range(BATCH)]
        outs["pp"].append(jnp.stack([lax.slice(proj, (e - POOL_BUF, 0), (e, POOL_WIDTH)) for e in ends]))
        outs["pc"].append(jnp.stack([lax.slice(proj, (e - (SSD_CONV - 1), POOL_WIDTH + SSD_WIDTH), (e, MAIN_WIDTH))
                                     for e in ends]))
        outs["ps"].append(jnp.transpose(st_p.reshape(BATCH, D_STATE, SSD_HEADS, SSD_HEAD_DIM), (0, 2, 3, 1)))
        outs["pf"].append(nffn_p[:BATCH])
        u_s = jnp.swapaxes(lax.slice(proj, (TP, 0), (T, POOL_WIDTH)).reshape(DEC_SEQ, DEC_BATCH, POOL_WIDTH), 0, 1)
        outs["sp"].append(jnp.concatenate([state_pool[l][:, DEC_SEQ:], u_s], axis=1))
        outs["sc"].append(jnp.swapaxes(nconv_s, 0, 1))
        outs["sf"].append(jnp.swapaxes(nffn_s, 0, 1))

    y_prompt = y_p.reshape(BATCH, SEQ, D_MODEL)
    y_sample = jnp.swapaxes(y_s.reshape(DEC_SEQ, DEC_BATCH, D_MODEL), 0, 1)
    return (y_prompt, y_sample,
            jnp.stack(outs["pp"]), jnp.stack(outs["pc"]), jnp.stack(outs["ps"]), jnp.stack(outs["pf"]),
            jnp.stack(outs["sp"]), jnp.stack(outs["sc"]), nssm_stack, jnp.stack(outs["sf"]))
```

```python
import functools

import jax
import jax.numpy as jnp
from jax import lax
from jax.experimental import pallas as pl
from jax.experimental.pallas import tpu as pltpu

F32 = jnp.float32
BF16 = jnp.bfloat16

D_MODEL = 2048
BATCH = 4
SEQ = 2048
DEPTH = 4
DEC_BATCH = 128
DEC_SEQ = 4
PAST_LEN = 16384

POOL_WIDTH = 512
POOL_WINDOWS = (2, 4, 8, 16)
POOL_GROUP_DIM = 128
POOL_BUF = 15
SSD_WIDTH = 1536
SSD_HEAD_DIM = 64
SSD_HEADS = 24
SSD_GROUPS = 4
SSD_HPG = 6
D_STATE = 128
SSD_CONV = 4
XBC_WIDTH = 2560
MAIN_WIDTH = POOL_WIDTH + SSD_WIDTH + XBC_WIDTH
D_FF = 5504
FFN_CONV = 3
EPS = 1e-6

TP = BATCH * SEQ
TS = DEC_SEQ * DEC_BATCH
T = TP + TS
GROUP_COLS = SSD_HPG * SSD_HEAD_DIM
LANES = 128
CHUNK = 128

VMEM_LIMIT = 56 * 1024 * 1024


def _silu(x):
    return x * jax.nn.sigmoid(x)


def _softplus(x):
    return jnp.maximum(x, 0.0) + jnp.log1p(jnp.exp(-jnp.abs(x)))


def _cparams(sem):
    return pltpu.CompilerParams(dimension_semantics=sem, vmem_limit_bytes=VMEM_LIMIT)


def _rowmod(p_ref, s_ref, is_sample, tm):
    s = s_ref[...]
    if tm > DEC_BATCH:
        s = jnp.concatenate([s] * (tm // DEC_BATCH), axis=0)
    return jnp.where(is_sample, s, p_ref[...])


def _rms_mod(x, gain, sc, sh):
    ms = jnp.mean(x * x, axis=-1, keepdims=True)
    y = x * lax.rsqrt(ms + EPS) * gain
    return y * (1.0 + sc) + sh


def _ada_kernel(c_ref, w_ref, b_ref, os_ref, op_ref):
    a = _silu(c_ref[...]).astype(BF16)
    m = jnp.dot(a, w_ref[...].astype(BF16), preferred_element_type=F32) + b_ref[...]
    os_ref[...] = m[:DEC_BATCH]
    op_ref[...] = m[DEC_BATCH:DEC_BATCH + BATCH]


def _ada(c_all, w_ada, b_ada):
    tn = 1024
    rows = c_all.shape[0]
    return pl.pallas_call(
        _ada_kernel,
        grid=(DEPTH, 6 * D_MODEL // tn),
        in_specs=[
            pl.BlockSpec((rows, D_MODEL), lambda l, j: (0, 0)),
            pl.BlockSpec((None, D_MODEL, tn), lambda l, j: (l, 0, j)),
            pl.BlockSpec((None, 1, tn), lambda l, j: (l, 0, j)),
        ],
        out_specs=[
            pl.BlockSpec((None, DEC_BATCH, tn), lambda l, j: (l, 0, j)),
            pl.BlockSpec((None, BATCH, tn), lambda l, j: (l, 0, j)),
        ],
        out_shape=[
            jax.ShapeDtypeStruct((DEPTH, DEC_BATCH, 6 * D_MODEL), F32),
            jax.ShapeDtypeStruct((DEPTH, BATCH, 6 * D_MODEL), F32),
        ],
        compiler_params=_cparams(("arbitrary", "arbitrary")),
        name="ada",
    )(c_all, w_ada, b_ada.reshape(DEPTH, 1, 6 * D_MODEL))


def _mod_specs(l, k, tm, row_of):
    tpb = SEQ // tm

    def pmap(*ids):
        return (l, jnp.minimum(row_of(*ids) // tpb, BATCH - 1), 0, k)

    def smap(*ids):
        return (l, 0, k)

    return (pl.BlockSpec((None, None, 1, D_MODEL), pmap), pl.BlockSpec((None, DEC_BATCH, D_MODEL), smap))


def _norm_kernel(x_ref, g_ref, scp, scs, shp, shs, o_ref, *, npt, tm):
    is_s = pl.program_id(0) >= npt
    sc = _rowmod(scp, scs, is_s, tm)
    sh = _rowmod(shp, shs, is_s, tm)
    o_ref[...] = _rms_mod(x_ref[...], g_ref[...], sc, sh).astype(o_ref.dtype)


def _norm0(x, gain, modp, mods, l):
    tm = 256
    npt = TP // tm
    row = lambda i: i
    scp, scs = _mod_specs(l, 1, tm, row)
    shp, shs = _mod_specs(l, 0, tm, row)
    return pl.pallas_call(
        functools.partial(_norm_kernel, npt=npt, tm=tm),
        grid=(T // tm,),
        in_specs=[pl.BlockSpec((tm, D_MODEL), lambda i: (i, 0)),
                  pl.BlockSpec((1, D_MODEL), lambda i: (0, 0)),
                  scp, scs, shp, shs],
        out_specs=pl.BlockSpec((tm, D_MODEL), lambda i: (i, 0)),
        out_shape=jax.ShapeDtypeStruct((T, D_MODEL), BF16),
        compiler_params=_cparams(("arbitrary",)),
        name="norm0",
    )(x, gain.reshape(1, D_MODEL), modp, mods, modp, mods)


_NT = (((1,), (1,)), ((), ()))


def _inproj_kernel(h_ref, w_ref, wdt_ref, o_ref, dt_ref, wb, wdtb):
    j = pl.program_id(0)
    m = pl.program_id(1)

    @pl.when(m == 0)
    def _():
        wb[...] = w_ref[...].astype(BF16)

    h = h_ref[...]
    o_ref[...] = lax.dot_general(h, wb[...], _NT, preferred_element_type=F32)

    @pl.when(j == 0)
    def _():
        @pl.when(m == 0)
        def _():
            row = lax.broadcasted_iota(jnp.int32, (LANES, 1), 0)
            wdtb[...] = jnp.where(row < SSD_HEADS, wdt_ref[...], 0.0).astype(BF16)

        dt_ref[...] = lax.dot_general(h, wdtb[...], _NT, preferred_element_type=F32)

    @pl.when(j != 0)
    def _():
        dt_ref[...] = jnp.zeros(dt_ref.shape, F32)


def _inproj(h, w_in_t, l):
    tm, tn = 512, 1536
    nm = T // tm
    return pl.pallas_call(
        _inproj_kernel,
        grid=(MAIN_WIDTH // tn, nm),
        in_specs=[pl.BlockSpec((tm, D_MODEL), lambda j, m: (m, 0)),
                  pl.BlockSpec((None, tn, D_MODEL), lambda j, m: (l, j, 0)),
                  pl.BlockSpec((None, LANES, D_MODEL), lambda j, m: (l, MAIN_WIDTH // LANES, 0))],
        out_specs=[pl.BlockSpec((tm, tn), lambda j, m: (m, j)),
                   pl.BlockSpec((tm, LANES), lambda j, m: (jnp.where(j == 0, m, nm), 0))],
        out_shape=[jax.ShapeDtypeStruct((T, MAIN_WIDTH), F32),
                   jax.ShapeDtypeStruct((T + tm, LANES), F32)],
        scratch_shapes=[pltpu.VMEM((tn, D_MODEL), BF16), pltpu.VMEM((LANES, D_MODEL), BF16)],
        compiler_params=_cparams(("arbitrary", "arbitrary")),
        name="inproj",
    )(h, w_in_t, w_in_t)


def _expand_heads(v, rows):
    lo = lax.broadcasted_iota(jnp.int32, (rows, LANES), 1) < SSD_HEAD_DIM
    parts = []
    for j in range(SSD_HEADS // 2):
        a = jnp.broadcast_to(v[:, 2 * j:2 * j + 1], (rows, LANES))
        b = jnp.broadcast_to(v[:, 2 * j + 1:2 * j + 2], (rows, LANES))
        parts.append(jnp.where(lo, a, b))
    return jnp.concatenate(parts, axis=1)


def _split3(x):
    a = x.astype(BF16)
    r = x - a.astype(F32)
    b = r.astype(BF16)
    c = (r - b.astype(F32)).astype(BF16)
    return a, b, c


def _gated_norm(y, z, gain):
    y = y * _silu(z)
    ms = jnp.mean(y * y, axis=-1, keepdims=True)
    return y * lax.rsqrt(ms + EPS) * gain


def _mixp_kernel(proj_ref, dt_ref, poolw_ref, pscale_ref, cw_ref, cb_ref, dtb_ref, alog_ref, dskip_ref, ng_ref, hc_ref,
                 mix_ref, st_ref, xpad, upad, hT):
    c = pl.program_id(1)
    Q = CHUNK

    @pl.when(c == 0)
    def _():
        xpad[0:8, :] = jnp.zeros((8, XBC_WIDTH), F32)
        upad[0:16, :] = jnp.zeros((16, POOL_WIDTH), F32)
        hT[...] = jnp.zeros(hT.shape, F32)

    u = proj_ref[:, 0:POOL_WIDTH]
    upad[16:16 + Q, :] = u
    pos = c * Q + lax.broadcasted_iota(jnp.int32, (Q, 1), 0)
    outs = []
    for g, w in enumerate(POOL_WINDOWS):
        lo = g * POOL_GROUP_DIM
        ug = u[:, lo:lo + POOL_GROUP_DIM]
        s = ug
        for k in range(1, w):
            s = s + upad[16 - k:16 - k + Q, lo:lo + POOL_GROUP_DIM]
        cnt = jnp.minimum(pos + 1, w).astype(F32)
        d = s / cnt - ug
        outs.append(jnp.dot(d.astype(BF16), poolw_ref[g].astype(BF16), preferred_element_type=F32))
    mix_ref[:, 0:POOL_WIDTH] = (jnp.concatenate(outs, axis=1) * pscale_ref[...]).astype(mix_ref.dtype)
    upad[0:16, :] = upad[Q:Q + 16, :]

    xbc = proj_ref[:, POOL_WIDTH + SSD_WIDTH:MAIN_WIDTH]
    xpad[8:8 + Q, :] = xbc
    acc = cb_ref[...] + cw_ref[3:4, :] * xbc
    for i in range(SSD_CONV - 1):
        acc = acc + cw_ref[i:i + 1, :] * xpad[5 + i:5 + i + Q, :]
    xpad[0:8, :] = xpad[Q:Q + 8, :]
    xc = _silu(acc)
    xs = xc[:, 0:SSD_WIDTH]
    Bb = xc[:, SSD_WIDTH:SSD_WIDTH + SSD_GROUPS * D_STATE].astype(BF16)
    Cb = xc[:, SSD_WIDTH + SSD_GROUPS * D_STATE:XBC_WIDTH].astype(BF16)

    dt = _softplus(dt_ref[...] + dtb_ref[...])
    dA = dt * (-jnp.exp(alog_ref[...]))
    row = lax.broadcasted_iota(jnp.int32, (Q, Q), 0)
    col = lax.broadcasted_iota(jnp.int32, (Q, Q), 1)
    tri = row >= col
    tri_lo = jnp.where(tri, 1.0, 0.0).astype(BF16)
    tri_up = jnp.where(row <= col, 1.0, 0.0).astype(BF16)
    pieces = _split3(dA)
    cs = sum(jnp.dot(tri_lo, p, preferred_element_type=F32) for p in pieces)
    csT = sum(lax.dot_general(p, tri_up, (((0,), (0,)), ((), ())), preferred_element_type=F32)
              for p in pieces)
    cs_last = cs[Q - 1:Q, :]
    stacked = jnp.concatenate([dt, jnp.exp(cs), jnp.exp(cs_last - cs)], axis=0)
    head_cols = hc_ref[...]
    expanded = sum(jnp.dot(p, head_cols, preferred_element_type=F32) for p in _split3(stacked))
    expcs_e = expanded[Q:2 * Q]
    dte_e = expanded[2 * Q:3 * Q]
    xdt = xs * expanded[0:Q]
    xdt_b = xdt.astype(BF16)
    lo_half = lax.broadcasted_iota(jnp.int32, (Q, LANES), 1) < SSD_HEAD_DIM
    xw_b = (xdt * dte_e).astype(BF16)
    decay_e = expcs_e[Q - 1:Q, :]
    h_prev = hT[...]
    h_prev_b = h_prev.astype(BF16)

    ys = []
    for g in range(SSD_GROUPS):
        gs = slice(g * GROUP_COLS, (g + 1) * GROUP_COLS)
        Cg = Cb[:, g * D_STATE:(g + 1) * D_STATE]
        Bg = Bb[:, g * D_STATE:(g + 1) * D_STATE]
        CB = lax.dot_general(Cg, Bg, (((1,), (1,)), ((), ())), preferred_element_type=F32)
        y_off = jnp.dot(Cg, h_prev_b[:, gs], preferred_element_type=F32)
        ST = lax.dot_general(Bg, xw_b[:, gs], (((0,), (0,)), ((), ())), preferred_element_type=F32)
        hT[:, gs] = h_prev[:, gs] * decay_e[:, gs] + ST
        parts = []
        for j in range(SSD_HPG // 2):
            h1 = SSD_HPG * g + 2 * j
            ps = slice(h1 * SSD_HEAD_DIM, h1 * SSD_HEAD_DIM + LANES)
            W = []
            for h in (h1, h1 + 1):
                seg = cs[:, h:h + 1] - csT[h:h + 1, :]
                W.append((CB * jnp.exp(jnp.where(tri, seg, -jnp.inf))).astype(BF16))
            parts.append(jnp.where(lo_half,
                                   jnp.dot(W[0], xdt_b[:, ps], preferred_element_type=F32),
                                   jnp.dot(W[1], xdt_b[:, ps], preferred_element_type=F32)))
        ys.append(jnp.concatenate(parts, axis=1) + y_off * expcs_e[:, gs])
    y = jnp.concatenate(ys, axis=1) + dskip_ref[...] * xs
    z = proj_ref[:, POOL_WIDTH:POOL_WIDTH + SSD_WIDTH]
    mix_ref[:, POOL_WIDTH:D_MODEL] = _gated_norm(y, z, ng_ref[...]).astype(mix_ref.dtype)

    @pl.when(c == pl.num_programs(1) - 1)
    def _():
        st_ref[...] = hT[...]


def _mix_prompt(proj, dtp, lw):
    nc = SEQ // CHUNK
    full = lambda shape: pl.BlockSpec(shape, lambda b, c: (0,) * len(shape))
    return pl.pallas_call(
        _mixp_kernel,
        grid=(BATCH, nc),
        in_specs=[pl.BlockSpec((CHUNK, MAIN_WIDTH), lambda b, c: (b * nc + c, 0)),
                  pl.BlockSpec((CHUNK, LANES), lambda b, c: (b * nc + c, 0)),
                  full((len(POOL_WINDOWS), POOL_GROUP_DIM, POOL_GROUP_DIM)),
                  full((1, POOL_WIDTH)),
                  full((SSD_CONV, XBC_WIDTH)),
                  full((1, XBC_WIDTH)),
                  full((1, LANES)), full((1, LANES)),
                  full((1, SSD_WIDTH)), full((1, SSD_WIDTH)),
                  full((LANES, SSD_WIDTH))],
        out_specs=[pl.BlockSpec((CHUNK, D_MODEL), lambda b, c: (b * nc + c, 0)),
                   pl.BlockSpec((None, D_STATE, SSD_WIDTH), lambda b, c: (b, 0, 0))],
        out_shape=[jax.ShapeDtypeStruct((TP, D_MODEL), BF16),
                   jax.ShapeDtypeStruct((BATCH, D_STATE, SSD_WIDTH), F32)],
        scratch_shapes=[pltpu.VMEM((8 + CHUNK, XBC_WIDTH), F32),
                        pltpu.VMEM((16 + CHUNK, POOL_WIDTH), F32),
                        pltpu.VMEM((D_STATE, SSD_WIDTH), F32)],
        compiler_params=_cparams(("arbitrary", "arbitrary")),
        name="mix_prompt",
    )(proj, dtp, lw["pool_w"], lw["pool_scale"], lw["conv_w"], lw["conv_b"], lw["dt_bias"], lw["a_log"],
      lw["d_skip_e"], lw["ssd_norm"], _head_cols())


def _head_cols():
    head_of_col = jnp.arange(SSD_WIDTH, dtype=jnp.int32) // SSD_HEAD_DIM
    return (head_of_col[None, :] == jnp.arange(LANES, dtype=jnp.int32)[:, None]).astype(BF16)


SAMPLE_BT = 8


def _mixs_kernel(*refs, aliased):
    if aliased:
        refs = refs[1:]
    (p0, p1, p2, p3, d0, d1, d2, d3, sconv_ref, spool_ref, sssm_ref,
     poolw_ref, pscale_ref, cw_ref, cb_ref, dtb_ref, alog_ref, dskip_ref, ng_ref,
     mix_ref, nconv_ref, nssm_ref, c_scr, b_scr, xw_scr, yoff_scr) = refs
    bt = SAMPLE_BT
    P = (p0, p1, p2, p3)
    Dt = (d0, d1, d2, d3)
    L = DEC_SEQ

    fullp = [spool_ref[k] for k in range(POOL_BUF)] + [P[t][:, 0:POOL_WIDTH] for t in range(L)]
    for g, w in enumerate(POOL_WINDOWS):
        gl = slice(g * POOL_GROUP_DIM, (g + 1) * POOL_GROUP_DIM)
        ds = []
        for t in range(L):
            s = fullp[POOL_BUF + t][:, gl]
            for k in range(1, w):
                s = s + fullp[POOL_BUF + t - k][:, gl]
            cnt = float(min(PAST_LEN + t + 1, w))
            ds.append(s / cnt - fullp[POOL_BUF + t][:, gl])
        o = jnp.dot(jnp.concatenate(ds, axis=0).astype(BF16), poolw_ref[g].astype(BF16),
                    preferred_element_type=F32) * pscale_ref[:, gl]
        for t in range(L):
            mix_ref[t, :, gl] = o[t * bt:(t + 1) * bt]

    fullc = [sconv_ref[k] for k in range(SSD_CONV - 1)] + [P[t][:, POOL_WIDTH + SSD_WIDTH:MAIN_WIDTH] for t in range(L)]
    for k in range(SSD_CONV - 1):
        nconv_ref[k] = fullc[L + k]
    xs, Bm, Cm = [], [], []
    for t in range(L):
        acc = cb_ref[...]
        for i in range(SSD_CONV):
            acc = acc + cw_ref[i:i + 1, :] * fullc[t + i]
        xc = _silu(acc)
        xs.append(xc[:, 0:SSD_WIDTH])
        Bm.append(xc[:, SSD_WIDTH:SSD_WIDTH + SSD_GROUPS * D_STATE])
        Cm.append(xc[:, SSD_WIDTH + SSD_GROUPS * D_STATE:XBC_WIDTH])

    A = -jnp.exp(alog_ref[...])
    dts = [_softplus(Dt[t][...] + dtb_ref[...]) for t in range(L)]
    cs = []
    for t in range(L):
        dA = dts[t] * A
        cs.append(dA if t == 0 else cs[t - 1] + dA)
    dec24 = jnp.exp(cs[L - 1])
    expcs_e = [_expand_heads(jnp.exp(cs[t]), bt) for t in range(L)]
    xdt = [xs[t] * _expand_heads(dts[t], bt) for t in range(L)]

    ydiag = []
    for t in range(L):
        acc = None
        for s in range(t + 1):
            cb_parts = []
            for g in range(SSD_GROUPS):
                sl = slice(g * D_STATE, (g + 1) * D_STATE)
                r = jnp.sum(Cm[t][:, sl] * Bm[s][:, sl], axis=-1, keepdims=True)
                cb_parts.append(jnp.broadcast_to(r, (bt, GROUP_COLS)))
            coef = jnp.concatenate(cb_parts, axis=1)
            if s < t:
                coef = coef * _expand_heads(jnp.exp(cs[t] - cs[s]), bt)
            term = coef * xdt[s]
            acc = term if acc is None else acc + term
        ydiag.append(acc)

    for t in range(L):
        c_scr[t] = Cm[t]
        b_scr[t] = Bm[t]
        xw_scr[t] = xdt[t] * _expand_heads(jnp.exp(cs[L - 1] - cs[t]), bt)
    zpad_s = jnp.zeros((8 - L, D_STATE), F32)
    zpad_x = jnp.zeros((8 - L, GROUP_COLS), F32)
    for b in range(bt):
        for g in range(SSD_GROUPS):
            sl = slice(g * D_STATE, (g + 1) * D_STATE)
            gs = slice(g * GROUP_COLS, (g + 1) * GROUP_COLS)
            C8 = jnp.concatenate([c_scr[t, b:b + 1, sl] for t in range(L)] + [zpad_s], axis=0)
            B8 = jnp.concatenate([b_scr[t, b:b + 1, sl] for t in range(L)] + [zpad_s], axis=0)
            X8 = jnp.concatenate([xw_scr[t, b:b + 1, gs] for t in range(L)] + [zpad_x], axis=0)
            h0 = sssm_ref[b, SSD_HPG * g:SSD_HPG * (g + 1)].reshape(GROUP_COLS, D_STATE)
            GT = lax.dot_general(C8.astype(BF16), h0.astype(BF16), (((1,), (1,)), ((), ())),
                                 preferred_element_type=F32)
            U = lax.dot_general(X8, B8, (((0,), (0,)), ((), ())), preferred_element_type=F32)
            for r in range(SSD_HPG):
                h = SSD_HPG * g + r
                dec = jnp.broadcast_to(dec24[b:b + 1, h:h + 1], (SSD_HEAD_DIM, D_STATE))
                rs = slice(r * SSD_HEAD_DIM, (r + 1) * SSD_HEAD_DIM)
                nssm_ref[b, h] = h0[rs] * dec + U[rs]
            for t in range(L):
                yoff_scr[t, b:b + 1, gs] = GT[t:t + 1, :]

    for t in range(L):
        y = ydiag[t] + expcs_e[t] * yoff_scr[t] + dskip_ref[...] * xs[t]
        z = P[t][:, POOL_WIDTH:POOL_WIDTH + SSD_WIDTH]
        mix_ref[t, :, POOL_WIDTH:D_MODEL] = _gated_norm(y, z, ng_ref[...])


def _mix_sample(proj, dtp, sconv_t, spool_t, sssm, lw, l, nssm_stack):
    bt = SAMPLE_BT
    nb = DEC_BATCH // bt
    base = TP // bt
    aliased = nssm_stack is not None

    def pspec(t, width):
        return pl.BlockSpec((bt, width), lambda i, t=t: (base + t * nb + i, 0))

    full = lambda shape: pl.BlockSpec(shape, lambda i: (0,) * len(shape))
    return pl.pallas_call(
        functools.partial(_mixs_kernel, aliased=aliased),
        grid=(nb,),
        input_output_aliases={0: 2} if aliased else {},
        in_specs=([pl.BlockSpec(memory_space=pl.ANY)] if aliased else [])
        + [pspec(t, MAIN_WIDTH) for t in range(DEC_SEQ)] + [pspec(t, LANES) for t in range(DEC_SEQ)] + [
            pl.BlockSpec((None, SSD_CONV - 1, bt, XBC_WIDTH), lambda i: (l, 0, i, 0)),
            pl.BlockSpec((None, POOL_BUF, bt, POOL_WIDTH), lambda i: (l, 0, i, 0)),
            pl.BlockSpec((None, bt, SSD_HEADS, SSD_HEAD_DIM, D_STATE), lambda i: (l, i, 0, 0, 0)),
            full((len(POOL_WINDOWS), POOL_GROUP_DIM, POOL_GROUP_DIM)),
            full((1, POOL_WIDTH)),
            full((SSD_CONV, XBC_WIDTH)),
            full((1, XBC_WIDTH)),
            full((1, LANES)), full((1, LANES)),
            full((1, SSD_WIDTH)), full((1, SSD_WIDTH))],
        out_specs=[pl.BlockSpec((DEC_SEQ, bt, D_MODEL), lambda i: (0, i, 0)),
                   pl.BlockSpec((SSD_CONV - 1, bt, XBC_WIDTH), lambda i: (0, i, 0)),
                   pl.BlockSpec((None, bt, SSD_HEADS, SSD_HEAD_DIM, D_STATE), lambda i: (l, i, 0, 0, 0))],
        out_shape=[jax.ShapeDtypeStruct((DEC_SEQ, DEC_BATCH, D_MODEL), F32),
                   jax.ShapeDtypeStruct((SSD_CONV - 1, DEC_BATCH, XBC_WIDTH), F32),
                   jax.ShapeDtypeStruct((DEPTH, DEC_BATCH, SSD_HEADS, SSD_HEAD_DIM, D_STATE), F32)],
        scratch_shapes=[pltpu.VMEM((DEC_SEQ, bt, SSD_GROUPS * D_STATE), F32),
                        pltpu.VMEM((DEC_SEQ, bt, SSD_GROUPS * D_STATE), F32),
                        pltpu.VMEM((DEC_SEQ, bt, SSD_WIDTH), F32),
                        pltpu.VMEM((DEC_SEQ, bt, SSD_WIDTH), F32)],
        compiler_params=_cparams(("arbitrary",)),
        name="mix_sample",
    )(*([nssm_stack] if aliased else []),
      proj, proj, proj, proj, dtp, dtp, dtp, dtp, sconv_t, spool_t, sssm,
      lw["pool_w"], lw["pool_scale"], lw["conv_w"], lw["conv_b"], lw["dt_bias"], lw["a_log"],
      lw["d_skip_e"], lw["ssd_norm"])


RESID_PIECE_ROWS = 256


def _row_pieces(tm):
    return [slice(r0, r0 + RESID_PIECE_ROWS) for r0 in range(0, tm, RESID_PIECE_ROWS)]


def _resid_epilogue(acc, x_ref, r, gp, gs, ng_ref, mods, is_s):
    rows = r.stop - r.start
    xn = x_ref[r, :] + _rowmod(gp, gs, is_s, rows) * acc
    if mods is None:
        ms = jnp.mean(xn * xn, axis=-1, keepdims=True)
        return xn, xn * lax.rsqrt(ms + EPS) * ng_ref[...]
    scp, scs, shp, shs = mods
    return xn, _rms_mod(xn, ng_ref[...], _rowmod(scp, scs, is_s, rows), _rowmod(shp, shs, is_s, rows))


def _outproj_kernel(ap_ref, as_ref, w_ref, x_ref, gp, gs, ng_ref, scp, scs, shp, shs, xo_ref, ho_ref, *, npt, tm):
    is_s = pl.program_id(0) >= npt
    for r in _row_pieces(tm):
        a = jnp.where(is_s, as_ref[r, :].astype(BF16), ap_ref[r, :])
        acc = jnp.dot(a, w_ref[...], preferred_element_type=F32)
        xn, hn = _resid_epilogue(acc, x_ref, r, gp, gs, ng_ref, (scp, scs, shp, shs), is_s)
        xo_ref[r, :] = xn
        ho_ref[r, :] = hn.astype(ho_ref.dtype)


def _down_kernel(a_ref, w_ref, x_ref, gp, gs, ng_ref, scp, scs, shp, shs, xo_ref, ho_ref, *, npt, tm):
    is_s = pl.program_id(0) >= npt
    for r in _row_pieces(tm):
        acc = jnp.dot(a_ref[r, :], w_ref[...], preferred_element_type=F32)
        xn, hn = _resid_epilogue(acc, x_ref, r, gp, gs, ng_ref, (scp, scs, shp, shs), is_s)
        xo_ref[r, :] = xn
        ho_ref[r, :] = hn.astype(ho_ref.dtype)


def _final_kernel(a_ref, w_ref, x_ref, gp, gs, ng_ref, yp_ref, ys_ref, *, npt, tm):
    is_s = pl.program_id(0) >= npt
    acc = jnp.dot(a_ref[...], w_ref[...], preferred_element_type=F32)
    _, yn = _resid_epilogue(acc, x_ref, slice(0, tm), gp, gs, ng_ref, None, is_s)

    @pl.when(jnp.logical_not(is_s))
    def _():
        yp_ref[...] = yn

    @pl.when(is_s)
    def _():
        ys_ref[...] = yn


def _resid_proj(a_list, w, x, gain, modp, mods, l, gate_k, next_mod):
    kdim = w.shape[-2]
    tm = 512 if kdim <= D_MODEL else 256
    npt = TP // tm
    row = lambda i: i
    if w.ndim == 3:
        w_spec = pl.BlockSpec((None, kdim, D_MODEL), lambda i: (l, 0, 0), pipeline_mode=pl.Buffered(1))
    else:
        w_spec = pl.BlockSpec((kdim, D_MODEL), lambda i: (0, 0), pipeline_mode=pl.Buffered(1))
    gp, gs = _mod_specs(l, gate_k, tm, row)
    if len(a_list) == 2:
        in_specs = [pl.BlockSpec((tm, kdim), lambda i: (jnp.minimum(i, npt - 1), 0)),
                    pl.BlockSpec((tm, kdim), lambda i: (jnp.maximum(i - npt, 0), 0))]
        kern = _outproj_kernel
    else:
        in_specs = [pl.BlockSpec((tm, kdim), lambda i: (i, 0))]
        kern = _down_kernel
    in_specs += [w_spec,
                 pl.BlockSpec((tm, D_MODEL), lambda i: (i, 0)),
                 gp, gs,
                 pl.BlockSpec((1, D_MODEL), lambda i: (0, 0))]
    args = list(a_list) + [w, x, modp, mods, gain.reshape(1, D_MODEL)]
    if next_mod is None:
        kern = _final_kernel
        out_specs = [pl.BlockSpec((tm, D_MODEL), lambda i: (jnp.minimum(i, npt - 1), 0)),
                     pl.BlockSpec((tm, D_MODEL), lambda i: (jnp.maximum(i - npt, 0), 0))]
        out_shape = [jax.ShapeDtypeStruct((TP, D_MODEL), F32), jax.ShapeDtypeStruct((TS, D_MODEL), F32)]
    else:
        nl, shift_k, scale_k = next_mod
        scp, scs = _mod_specs(nl, scale_k, tm, row)
        shp, shs = _mod_specs(nl, shift_k, tm, row)
        in_specs += [scp, scs, shp, shs]
        args += [modp, mods, modp, mods]
        out_specs = [pl.BlockSpec((tm, D_MODEL), lambda i: (i, 0)),
                     pl.BlockSpec((tm, D_MODEL), lambda i: (i, 0))]
        out_shape = [jax.ShapeDtypeStruct((T, D_MODEL), F32), jax.ShapeDtypeStruct((T, D_MODEL), BF16)]
    return pl.pallas_call(
        functools.partial(kern, npt=npt, tm=tm),
        grid=(T // tm,),
        in_specs=in_specs,
        out_specs=out_specs,
        out_shape=out_shape,
        compiler_params=_cparams(("arbitrary",)),
        name="resid_proj",
    )(*args)


def _up_kernel(h_ref, wg_ref, wv0, wv1, wv2, wv3, cw_ref, cb_ref, sf_ref, wd_ref,
               act_ref, nfp_ref, nfs_ref, wdb_ref, wgb, wvb, gpad, *, npt, tpb, tm, tn):
    m = pl.program_id(1)

    @pl.when(m == 0)
    def _():
        wgb[...] = wg_ref[...].astype(BF16)
        for k, r in enumerate((wv0, wv1, wv2, wv3)):
            wvb[:, k * LANES:(k + 1) * LANES] = r[...].astype(BF16)
        wdb_ref[...] = wd_ref[...].astype(BF16)

    is_s = m >= npt
    first = (m % tpb) == 0
    hw = tn // 2
    B1, B2 = DEC_BATCH, 2 * DEC_BATCH
    plan = [(0, tm // 2, 0), (tm // 2, tm // 2, 0),
            (0, tm // 2, 1), (tm // 2, tm // 4, 1), (3 * tm // 4, tm // 4, 1)]
    h_halves = (h_ref[0:tm // 2, :], h_ref[tm // 2:tm, :])
    for r0, rows, half in plan:
        cs = slice(half * hw, (half + 1) * hw)
        hc = h_halves[r0 // (tm // 2)]
        if rows < tm // 2:
            q0 = r0 % (tm // 2)
            hc = hc[q0:q0 + rows]
        gate = jnp.dot(hc, wgb[:, cs], preferred_element_type=F32)
        val = jnp.dot(hc, wvb[:, cs], preferred_element_type=F32)
        if r0 == 0:
            gpad[0:8, cs] = jnp.where(first, 0.0, gpad[0:8, cs])
        gpad[8 + r0:8 + r0 + rows, cs] = gate
        gm1 = gpad[7 + r0:7 + r0 + rows, cs]
        gm2 = gpad[6 + r0:6 + r0 + rows, cs]
        if r0 < TS:
            if r0 == 0:
                s1 = jnp.concatenate([sf_ref[1, :, cs], gpad[8:8 + rows - B1, cs]], axis=0)
                s2 = jnp.concatenate([sf_ref[0, :, cs], sf_ref[1, :, cs], gpad[8:8 + rows - B2, cs]], axis=0)
            else:
                s1 = gpad[8 + r0 - B1:8 + r0 - B1 + rows, cs]
                s2 = gpad[8 + r0 - B2:8 + r0 - B2 + rows, cs]
            gm1 = jnp.where(is_s, s1, gm1)
            gm2 = jnp.where(is_s, s2, gm2)
        gc = cb_ref[:, cs] + cw_ref[2:3, cs] * gate + cw_ref[1:2, cs] * gm1 + cw_ref[0:1, cs] * gm2
        act_ref[r0:r0 + rows, cs] = (_silu(gc) * val).astype(act_ref.dtype)
        if r0 + rows == tm:
            gpad[0:8, cs] = gpad[tm:tm + 8, cs]
            nfp_ref[:, cs] = jnp.where(is_s, 0.0, gate[rows - (FFN_CONV - 1):rows])
        if r0 == 0:
            nfs_ref[0, :, cs] = gate[(DEC_SEQ - 2) * B1:(DEC_SEQ - 1) * B1]
            nfs_ref[1, :, cs] = gate[(DEC_SEQ - 1) * B1:DEC_SEQ * B1]


def _ffn_up(h2, w_up, conv_w, conv_b, sffn_t, w_down, l):
    tm, tn = 1024, 512
    assert TS % (tm // 4) == 0 and TS <= tm // 2 and SEQ % tm == 0
    npt = TP // tm
    tpb = SEQ // tm
    nn = pl.cdiv(D_FF, tn)
    val0 = D_FF // LANES
    last = 2 * D_FF // LANES - 1

    def vspec(k):
        return pl.BlockSpec((None, D_MODEL, LANES),
                            lambda j, m: (l, 0, jnp.minimum(val0 + (tn // LANES) * j + k, last)))

    return pl.pallas_call(
        functools.partial(_up_kernel, npt=npt, tpb=tpb, tm=tm, tn=tn),
        grid=(nn, pl.cdiv(T, tm)),
        in_specs=[pl.BlockSpec((tm, D_MODEL), lambda j, m: (m, 0)),
                  pl.BlockSpec((None, D_MODEL, tn), lambda j, m: (l, 0, j))]
                 + [vspec(k) for k in range(tn // LANES)]
                 + [pl.BlockSpec((FFN_CONV, tn), lambda j, m: (0, j)),
                    pl.BlockSpec((1, tn), lambda j, m: (0, j)),
                    pl.BlockSpec((None, FFN_CONV - 1, DEC_BATCH, tn), lambda j, m: (l, 0, 0, j)),
                    pl.BlockSpec((None, tn, D_MODEL), lambda j, m: (l, j, 0))],
        out_specs=[pl.BlockSpec((tm, tn), lambda j, m: (m, j)),
                   pl.BlockSpec((None, FFN_CONV - 1, tn), lambda j, m: (jnp.minimum(m // tpb, BATCH), 0, j)),
                   pl.BlockSpec((FFN_CONV - 1, DEC_BATCH, tn), lambda j, m: (0, 0, j)),
                   pl.BlockSpec((tn, D_MODEL), lambda j, m: (j, 0))],
        out_shape=[jax.ShapeDtypeStruct((T, D_FF), BF16),
                   jax.ShapeDtypeStruct((BATCH + 1, FFN_CONV - 1, D_FF), F32),
                   jax.ShapeDtypeStruct((FFN_CONV - 1, DEC_BATCH, D_FF), F32),
                   jax.ShapeDtypeStruct((D_FF, D_MODEL), BF16)],
        scratch_shapes=[pltpu.VMEM((D_MODEL, tn), BF16), pltpu.VMEM((D_MODEL, tn), BF16),
                        pltpu.VMEM((8 + tm, tn), F32)],
        compiler_params=_cparams(("arbitrary", "arbitrary")),
        name="ffn_up",
    )(h2, w_up, w_up, w_up, w_up, w_up, conv_w, conv_b.reshape(1, D_FF), sffn_t, w_down)


def _pad_lanes(v):
    return jnp.pad(v, ((0, 0), (0, LANES - v.shape[-1])))


def kernel(x_prompt, x_sample, c_prompt, c_sample, state_pool, state_conv, state_ssm, state_ffn, norm1, norm2, w_ada, b_ada, w_in, pool_w, pool_scale, conv_w, conv_b, dt_bias, a_log, d_skip, ssd_norm, w_out, w_up, ffn_conv_w, ffn_conv_b, w_down, norm_f):
    x = jnp.concatenate([x_prompt.reshape(TP, D_MODEL),
                         jnp.swapaxes(x_sample, 0, 1).reshape(TS, D_MODEL)], axis=0)
    c_all = jnp.concatenate([c_sample, c_prompt, jnp.zeros((4, D_MODEL), F32)], axis=0)
    mods, modp = _ada(c_all, w_ada, b_ada)
    modp = modp.reshape(DEPTH, BATCH, 1, 6 * D_MODEL)

    spool_t = jnp.swapaxes(state_pool, 1, 2)
    sconv_t = jnp.swapaxes(state_conv, 1, 2)
    sffn_t = jnp.swapaxes(state_ffn, 1, 2)

    w_in_t = jnp.swapaxes(w_in, 1, 2)
    w_out_b = w_out.astype(BF16)

    d_skip_e = jnp.repeat(d_skip, SSD_HEAD_DIM, axis=-1)

    outs = {k: [] for k in ("pp", "pc", "ps", "pf", "sp", "sc", "sf")}
    h = _norm0(x, norm1[0], modp, mods, 0)
    nssm_stack = None
    for l in range(DEPTH):
        lw = {"pool_w": pool_w[l], "pool_scale": pool_scale[l].reshape(1, POOL_WIDTH),
              "conv_w": conv_w[l], "conv_b": conv_b[l].reshape(1, XBC_WIDTH),
              "dt_bias": _pad_lanes(dt_bias[l].reshape(1, SSD_HEADS)),
              "a_log": _pad_lanes(a_log[l].reshape(1, SSD_HEADS)),
              "d_skip_e": d_skip_e[l].reshape(1, SSD_WIDTH),
              "ssd_norm": ssd_norm[l].reshape(1, SSD_WIDTH)}
        proj, dtp = _inproj(h, w_in_t, l)
        mix_p, st_p = _mix_prompt(proj, dtp, lw)
        mix_s, nconv_s, nssm_stack = _mix_sample(proj, dtp, sconv_t, spool_t, state_ssm, lw, l, nssm_stack)
        x1, h2 = _resid_proj([mix_p, mix_s.reshape(TS, D_MODEL)], w_out_b, x, norm2[l], modp, mods, l, 2, (l, 3, 4))
        act, nffn_p, nffn_s, w_down_b = _ffn_up(h2, w_up, ffn_conv_w[l], ffn_conv_b[l], sffn_t, w_down, l)
        if l + 1 < DEPTH:
            x, h = _resid_proj([act], w_down_b, x1, norm1[l + 1], modp, mods, l, 5, (l + 1, 0, 1))
        else:
            y_p, y_s = _resid_proj([act], w_down_b, x1, norm_f, modp, mods, l, 5, None)

        ends = [(b + 1) * SEQ for b in range(BATCH)]
        outs["pp"].append(jnp.stack([lax.slice(proj, (e - POOL_BUF, 0), (e, POOL_WIDTH)) for e in ends]))
        outs["pc"].append(jnp.stack([lax.slice(proj, (e - (SSD_CONV - 1), POOL_WIDTH + SSD_WIDTH), (e, MAIN_WIDTH))
                                     for e in ends]))
        outs["ps"].append(jnp.transpose(st_p.reshape(BATCH, D_STATE, SSD_HEADS, SSD_HEAD_DIM), (0, 2, 3, 1)))
        outs["pf"].append(nffn_p[:BATCH])
        u_s = jnp.swapaxes(lax.slice(proj, (TP, 0), (T, POOL_WIDTH)).reshape(DEC_SEQ, DEC_BATCH, POOL_WIDTH), 0, 1)
        outs["sp"].append(jnp.concatenate([state_pool[l][:, DEC_SEQ:], u_s], axis=1))
        outs["sc"].append(jnp.swapaxes(nconv_s, 0, 1))
        outs["sf"].append(jnp.swapaxes(nffn_s, 0, 1))

    y_prompt = y_p.reshape(BATCH, SEQ, D_MODEL)
    y_sample = jnp.swapaxes(y_s.reshape(DEC_SEQ, DEC_BATCH, D_MODEL), 0, 1)
    return (y_prompt, y_sample,
            jnp.stack(outs["pp"]), jnp.stack(outs["pc"]), jnp.stack(outs["ps"]), jnp.stack(outs["pf"]),
            jnp.stack(outs["sp"]), jnp.stack(outs["sc"]), nssm_stack, jnp.stack(outs["sf"]))
```

```python
import functools

import jax
import jax.numpy as jnp
from jax import lax
from jax.experimental import pallas as pl
from jax.experimental.pallas import tpu as pltpu

F32 = jnp.float32
BF16 = jnp.bfloat16

D_MODEL = 2048
BATCH = 4
SEQ = 2048
DEPTH = 4
DEC_BATCH = 128
DEC_SEQ = 4
PAST_LEN = 16384

POOL_WIDTH = 512
POOL_WINDOWS = (2, 4, 8, 16)
POOL_GROUP_DIM = 128
POOL_BUF = 15
SSD_WIDTH = 1536
SSD_HEAD_DIM = 64
SSD_HEADS = 24
SSD_GROUPS = 4
SSD_HPG = 6
D_STATE = 128
SSD_CONV = 4
XBC_WIDTH = 2560
MAIN_WIDTH = POOL_WIDTH + SSD_WIDTH + XBC_WIDTH
D_FF = 5504
FFN_CONV = 3
EPS = 1e-6

TP = BATCH * SEQ
TS = DEC_SEQ * DEC_BATCH
T = TP + TS
GROUP_COLS = SSD_HPG * SSD_HEAD_DIM
LANES = 128
CHUNK = 128

VMEM_LIMIT = 56 * 1024 * 1024


def _silu(x):
    return x * jax.nn.sigmoid(x)


def _softplus(x):
    return jnp.maximum(x, 0.0) + jnp.log1p(jnp.exp(-jnp.abs(x)))


def _cparams(sem):
    return pltpu.CompilerParams(dimension_semantics=sem, vmem_limit_bytes=VMEM_LIMIT)


def _rowmod(p_ref, s_ref, is_sample, tm):
    s = s_ref[...]
    if tm > DEC_BATCH:
        s = jnp.concatenate([s] * (tm // DEC_BATCH), axis=0)
    return jnp.where(is_sample, s, p_ref[...])


def _rms_mod(x, gain, sc, sh):
    ms = jnp.mean(x * x, axis=-1, keepdims=True)
    y = x * lax.rsqrt(ms + EPS) * gain
    return y * (1.0 + sc) + sh


def _ada_kernel(c_ref, w_ref, b_ref, os_ref, op_ref):
    a = _silu(c_ref[...]).astype(BF16)
    m = jnp.dot(a, w_ref[...].astype(BF16), preferred_element_type=F32) + b_ref[...]
    os_ref[...] = m[:DEC_BATCH]
    op_ref[...] = m[DEC_BATCH:DEC_BATCH + BATCH]


def _ada(c_all, w_ada, b_ada):
    tn = 1024
    rows = c_all.shape[0]
    return pl.pallas_call(
        _ada_kernel,
        grid=(DEPTH, 6 * D_MODEL // tn),
        in_specs=[
            pl.BlockSpec((rows, D_MODEL), lambda l, j: (0, 0)),
            pl.BlockSpec((None, D_MODEL, tn), lambda l, j: (l, 0, j)),
            pl.BlockSpec((None, 1, tn), lambda l, j: (l, 0, j)),
        ],
        out_specs=[
            pl.BlockSpec((None, DEC_BATCH, tn), lambda l, j: (l, 0, j)),
            pl.BlockSpec((None, BATCH, tn), lambda l, j: (l, 0, j)),
        ],
        out_shape=[
            jax.ShapeDtypeStruct((DEPTH, DEC_BATCH, 6 * D_MODEL), F32),
            jax.ShapeDtypeStruct((DEPTH, BATCH, 6 * D_MODEL), F32),
        ],
        compiler_params=_cparams(("arbitrary", "arbitrary")),
        name="ada",
    )(c_all, w_ada, b_ada.reshape(DEPTH, 1, 6 * D_MODEL))


def _mod_specs(l, k, tm, row_of):
    tpb = SEQ // tm

    def pmap(*ids):
        return (l, jnp.minimum(row_of(*ids) // tpb, BATCH - 1), 0, k)

    def smap(*ids):
        return (l, 0, k)

    return (pl.BlockSpec((None, None, 1, D_MODEL), pmap), pl.BlockSpec((None, DEC_BATCH, D_MODEL), smap))


def _norm_kernel(x_ref, g_ref, scp, scs, shp, shs, o_ref, *, npt, tm):
    is_s = pl.program_id(0) >= npt
    sc = _rowmod(scp, scs, is_s, tm)
    sh = _rowmod(shp, shs, is_s, tm)
    o_ref[...] = _rms_mod(x_ref[...], g_ref[...], sc, sh).astype(o_ref.dtype)


def _norm0(x, gain, modp, mods, l):
    tm = 256
    npt = TP // tm
    row = lambda i: i
    scp, scs = _mod_specs(l, 1, tm, row)
    shp, shs = _mod_specs(l, 0, tm, row)
    return pl.pallas_call(
        functools.partial(_norm_kernel, npt=npt, tm=tm),
        grid=(T // tm,),
        in_specs=[pl.BlockSpec((tm, D_MODEL), lambda i: (i, 0)),
                  pl.BlockSpec((1, D_MODEL), lambda i: (0, 0)),
                  scp, scs, shp, shs],
        out_specs=pl.BlockSpec((tm, D_MODEL), lambda i: (i, 0)),
        out_shape=jax.ShapeDtypeStruct((T, D_MODEL), BF16),
        compiler_params=_cparams(("arbitrary",)),
        name="norm0",
    )(x, gain.reshape(1, D_MODEL), modp, mods, modp, mods)


_NT = (((1,), (1,)), ((), ()))


def _inproj_kernel(h_ref, w_ref, wdt_ref, o_ref, dt_ref, wb, wdtb):
    j = pl.program_id(0)
    m = pl.program_id(1)

    @pl.when(m == 0)
    def _():
        wb[...] = w_ref[...].astype(BF16)

    h = h_ref[...]
    o_ref[...] = lax.dot_general(h, wb[...], _NT, preferred_element_type=F32)

    @pl.when(j == 0)
    def _():
        @pl.when(m == 0)
        def _():
            row = lax.broadcasted_iota(jnp.int32, (LANES, 1), 0)
            wdtb[...] = jnp.where(row < SSD_HEADS, wdt_ref[...], 0.0).astype(BF16)

        dt_ref[...] = lax.dot_general(h, wdtb[...], _NT, preferred_element_type=F32)

    @pl.when(j != 0)
    def _():
        dt_ref[...] = jnp.zeros(dt_ref.shape, F32)


def _inproj(h, w_in_t, l):
    tm, tn = 512, 1536
    nm = T // tm
    return pl.pallas_call(
        _inproj_kernel,
        grid=(MAIN_WIDTH // tn, nm),
        in_specs=[pl.BlockSpec((tm, D_MODEL), lambda j, m: (m, 0)),
                  pl.BlockSpec((None, tn, D_MODEL), lambda j, m: (l, j, 0)),
                  pl.BlockSpec((None, LANES, D_MODEL), lambda j, m: (l, MAIN_WIDTH // LANES, 0))],
        out_specs=[pl.BlockSpec((tm, tn), lambda j, m: (m, j)),
                   pl.BlockSpec((tm, LANES), lambda j, m: (jnp.where(j == 0, m, nm), 0))],
        out_shape=[jax.ShapeDtypeStruct((T, MAIN_WIDTH), F32),
                   jax.ShapeDtypeStruct((T + tm, LANES), F32)],
        scratch_shapes=[pltpu.VMEM((tn, D_MODEL), BF16), pltpu.VMEM((LANES, D_MODEL), BF16)],
        compiler_params=_cparams(("arbitrary", "arbitrary")),
        name="inproj",
    )(h, w_in_t, w_in_t)


def _expand_heads(v, rows):
    lo = lax.broadcasted_iota(jnp.int32, (rows, LANES), 1) < SSD_HEAD_DIM
    parts = []
    for j in range(SSD_HEADS // 2):
        a = jnp.broadcast_to(v[:, 2 * j:2 * j + 1], (rows, LANES))
        b = jnp.broadcast_to(v[:, 2 * j + 1:2 * j + 2], (rows, LANES))
        parts.append(jnp.where(lo, a, b))
    return jnp.concatenate(parts, axis=1)


def _split3(x):
    a = x.astype(BF16)
    r = x - a.astype(F32)
    b = r.astype(BF16)
    c = (r - b.astype(F32)).astype(BF16)
    return a, b, c


def _gated_norm(y, z, gain):
    y = y * _silu(z)
    ms = jnp.mean(y * y, axis=-1, keepdims=True)
    return y * lax.rsqrt(ms + EPS) * gain


PROMPT_CHUNKS_PER_STEP = 2


def _mixp_kernel(proj_ref, dt_ref, poolw_ref, pscale_ref, cw_ref, cb_ref, dtb_ref, alog_ref, dskip_ref, ng_ref, hc_ref,
                 wo_ref, mix_ref, st_ref, wob_ref, xpad, upad, hT):
    c = pl.program_id(1)

    @pl.when(c == 0)
    def _():
        xpad[0:8, :] = jnp.zeros((8, XBC_WIDTH), F32)
        upad[0:16, :] = jnp.zeros((16, POOL_WIDTH), F32)
        hT[...] = jnp.zeros(hT.shape, F32)

    wob_ref[...] = wo_ref[...].astype(BF16)

    for sub in range(PROMPT_CHUNKS_PER_STEP):
        _mixp_chunk(c * PROMPT_CHUNKS_PER_STEP + sub, slice(sub * CHUNK, (sub + 1) * CHUNK),
                    proj_ref, dt_ref, poolw_ref, pscale_ref, cw_ref, cb_ref, dtb_ref, alog_ref, dskip_ref, ng_ref,
                    hc_ref, mix_ref, xpad, upad, hT)

    @pl.when(c == pl.num_programs(1) - 1)
    def _():
        st_ref[...] = hT[...]


def _mixp_chunk(chunk, r, proj_ref, dt_ref, poolw_ref, pscale_ref, cw_ref, cb_ref, dtb_ref, alog_ref, dskip_ref, ng_ref,
                hc_ref, mix_ref, xpad, upad, hT):
    Q = CHUNK

    u = proj_ref[r, 0:POOL_WIDTH]
    upad[16:16 + Q, :] = u
    pos = chunk * Q + lax.broadcasted_iota(jnp.int32, (Q, 1), 0)
    outs = []
    for g, w in enumerate(POOL_WINDOWS):
        lo = g * POOL_GROUP_DIM
        ug = u[:, lo:lo + POOL_GROUP_DIM]
        s = ug
        for k in range(1, w):
            s = s + upad[16 - k:16 - k + Q, lo:lo + POOL_GROUP_DIM]
        cnt = jnp.minimum(pos + 1, w).astype(F32)
        d = s / cnt - ug
        outs.append(jnp.dot(d.astype(BF16), poolw_ref[g].astype(BF16), preferred_element_type=F32))
    mix_ref[r, 0:POOL_WIDTH] = (jnp.concatenate(outs, axis=1) * pscale_ref[...]).astype(mix_ref.dtype)
    upad[0:16, :] = upad[Q:Q + 16, :]

    xbc = proj_ref[r, POOL_WIDTH + SSD_WIDTH:MAIN_WIDTH]
    xpad[8:8 + Q, :] = xbc
    acc = cb_ref[...] + cw_ref[3:4, :] * xbc
    for i in range(SSD_CONV - 1):
        acc = acc + cw_ref[i:i + 1, :] * xpad[5 + i:5 + i + Q, :]
    xpad[0:8, :] = xpad[Q:Q + 8, :]
    xc = _silu(acc)
    xs = xc[:, 0:SSD_WIDTH]
    Bb = xc[:, SSD_WIDTH:SSD_WIDTH + SSD_GROUPS * D_STATE].astype(BF16)
    Cb = xc[:, SSD_WIDTH + SSD_GROUPS * D_STATE:XBC_WIDTH].astype(BF16)

    dt = _softplus(dt_ref[r, :] + dtb_ref[...])
    dA = dt * (-jnp.exp(alog_ref[...]))
    row = lax.broadcasted_iota(jnp.int32, (Q, Q), 0)
    col = lax.broadcasted_iota(jnp.int32, (Q, Q), 1)
    tri = row >= col
    tri_lo = jnp.where(tri, 1.0, 0.0).astype(BF16)
    tri_up = jnp.where(row <= col, 1.0, 0.0).astype(BF16)
    pieces = _split3(dA)
    cs = sum(jnp.dot(tri_lo, p, preferred_element_type=F32) for p in pieces)
    csT = sum(lax.dot_general(p, tri_up, (((0,), (0,)), ((), ())), preferred_element_type=F32)
              for p in pieces)
    cs_last = cs[Q - 1:Q, :]
    stacked = jnp.concatenate([dt, jnp.exp(cs), jnp.exp(cs_last - cs)], axis=0)
    head_cols = hc_ref[...]
    expanded = sum(jnp.dot(p, head_cols, preferred_element_type=F32) for p in _split3(stacked))
    expcs_e = expanded[Q:2 * Q]
    dte_e = expanded[2 * Q:3 * Q]
    xdt = xs * expanded[0:Q]
    xdt_b = xdt.astype(BF16)
    lo_half = lax.broadcasted_iota(jnp.int32, (Q, LANES), 1) < SSD_HEAD_DIM
    xw_b = (xdt * dte_e).astype(BF16)
    decay_e = expcs_e[Q - 1:Q, :]
    h_prev = hT[...]
    h_prev_b = h_prev.astype(BF16)

    ys = []
    for g in range(SSD_GROUPS):
        gs = slice(g * GROUP_COLS, (g + 1) * GROUP_COLS)
        Cg = Cb[:, g * D_STATE:(g + 1) * D_STATE]
        Bg = Bb[:, g * D_STATE:(g + 1) * D_STATE]
        CB = lax.dot_general(Cg, Bg, (((1,), (1,)), ((), ())), preferred_element_type=F32)
        y_off = jnp.dot(Cg, h_prev_b[:, gs], preferred_element_type=F32)
        ST = lax.dot_general(Bg, xw_b[:, gs], (((0,), (0,)), ((), ())), preferred_element_type=F32)
        hT[:, gs] = h_prev[:, gs] * decay_e[:, gs] + ST
        parts = []
        for j in range(SSD_HPG // 2):
            h1 = SSD_HPG * g + 2 * j
            ps = slice(h1 * SSD_HEAD_DIM, h1 * SSD_HEAD_DIM + LANES)
            W = []
            for h in (h1, h1 + 1):
                seg = cs[:, h:h + 1] - csT[h:h + 1, :]
                W.append((CB * jnp.exp(jnp.where(tri, seg, -jnp.inf))).astype(BF16))
            parts.append(jnp.where(lo_half,
                                   jnp.dot(W[0], xdt_b[:, ps], preferred_element_type=F32),
                                   jnp.dot(W[1], xdt_b[:, ps], preferred_element_type=F32)))
        ys.append(jnp.concatenate(parts, axis=1) + y_off * expcs_e[:, gs])
    y = jnp.concatenate(ys, axis=1) + dskip_ref[...] * xs
    z = proj_ref[r, POOL_WIDTH:POOL_WIDTH + SSD_WIDTH]
    mix_ref[r, POOL_WIDTH:D_MODEL] = _gated_norm(y, z, ng_ref[...]).astype(mix_ref.dtype)


def _mix_prompt(proj, dtp, lw, w_out, l):
    rows = CHUNK * PROMPT_CHUNKS_PER_STEP
    nc = SEQ // rows
    wrows = D_MODEL // (BATCH * nc)
    full = lambda shape: pl.BlockSpec(shape, lambda b, c: (0,) * len(shape))
    return pl.pallas_call(
        _mixp_kernel,
        grid=(BATCH, nc),
        in_specs=[pl.BlockSpec((rows, MAIN_WIDTH), lambda b, c: (b * nc + c, 0)),
                  pl.BlockSpec((rows, LANES), lambda b, c: (b * nc + c, 0)),
                  full((len(POOL_WINDOWS), POOL_GROUP_DIM, POOL_GROUP_DIM)),
                  full((1, POOL_WIDTH)),
                  full((SSD_CONV, XBC_WIDTH)),
                  full((1, XBC_WIDTH)),
                  full((1, LANES)), full((1, LANES)),
                  full((1, SSD_WIDTH)), full((1, SSD_WIDTH)),
                  full((LANES, SSD_WIDTH)),
                  pl.BlockSpec((None, wrows, D_MODEL), lambda b, c: (l, b * nc + c, 0))],
        out_specs=[pl.BlockSpec((rows, D_MODEL), lambda b, c: (b * nc + c, 0)),
                   pl.BlockSpec((None, D_STATE, SSD_WIDTH), lambda b, c: (b, 0, 0)),
                   pl.BlockSpec((wrows, D_MODEL), lambda b, c: (b * nc + c, 0))],
        out_shape=[jax.ShapeDtypeStruct((TP, D_MODEL), BF16),
                   jax.ShapeDtypeStruct((BATCH, D_STATE, SSD_WIDTH), F32),
                   jax.ShapeDtypeStruct((D_MODEL, D_MODEL), BF16)],
        scratch_shapes=[pltpu.VMEM((8 + CHUNK, XBC_WIDTH), F32),
                        pltpu.VMEM((16 + CHUNK, POOL_WIDTH), F32),
                        pltpu.VMEM((D_STATE, SSD_WIDTH), F32)],
        compiler_params=_cparams(("arbitrary", "arbitrary")),
        name="mix_prompt",
    )(proj, dtp, lw["pool_w"], lw["pool_scale"], lw["conv_w"], lw["conv_b"], lw["dt_bias"], lw["a_log"],
      lw["d_skip_e"], lw["ssd_norm"], _head_cols(), w_out)


def _head_cols():
    head_of_col = jnp.arange(SSD_WIDTH, dtype=jnp.int32) // SSD_HEAD_DIM
    return (head_of_col[None, :] == jnp.arange(LANES, dtype=jnp.int32)[:, None]).astype(BF16)


SAMPLE_BT = 8


def _mixs_kernel(*refs, aliased):
    if aliased:
        refs = refs[1:]
    (p0, p1, p2, p3, d0, d1, d2, d3, sconv_ref, spool_ref, sssm_ref,
     poolw_ref, pscale_ref, cw_ref, cb_ref, dtb_ref, alog_ref, dskip_ref, ng_ref,
     mix_ref, nconv_ref, nssm_ref, c_scr, b_scr, xw_scr, yoff_scr) = refs
    bt = SAMPLE_BT
    P = (p0, p1, p2, p3)
    Dt = (d0, d1, d2, d3)
    L = DEC_SEQ

    fullp = [spool_ref[k] for k in range(POOL_BUF)] + [P[t][:, 0:POOL_WIDTH] for t in range(L)]
    for g, w in enumerate(POOL_WINDOWS):
        gl = slice(g * POOL_GROUP_DIM, (g + 1) * POOL_GROUP_DIM)
        ds = []
        for t in range(L):
            s = fullp[POOL_BUF + t][:, gl]
            for k in range(1, w):
                s = s + fullp[POOL_BUF + t - k][:, gl]
            cnt = float(min(PAST_LEN + t + 1, w))
            ds.append(s / cnt - fullp[POOL_BUF + t][:, gl])
        o = jnp.dot(jnp.concatenate(ds, axis=0).astype(BF16), poolw_ref[g].astype(BF16),
                    preferred_element_type=F32) * pscale_ref[:, gl]
        for t in range(L):
            mix_ref[t, :, gl] = o[t * bt:(t + 1) * bt]

    fullc = [sconv_ref[k] for k in range(SSD_CONV - 1)] + [P[t][:, POOL_WIDTH + SSD_WIDTH:MAIN_WIDTH] for t in range(L)]
    for k in range(SSD_CONV - 1):
        nconv_ref[k] = fullc[L + k]
    xs, Bm, Cm = [], [], []
    for t in range(L):
        acc = cb_ref[...]
        for i in range(SSD_CONV):
            acc = acc + cw_ref[i:i + 1, :] * fullc[t + i]
        xc = _silu(acc)
        xs.append(xc[:, 0:SSD_WIDTH])
        Bm.append(xc[:, SSD_WIDTH:SSD_WIDTH + SSD_GROUPS * D_STATE])
        Cm.append(xc[:, SSD_WIDTH + SSD_GROUPS * D_STATE:XBC_WIDTH])

    A = -jnp.exp(alog_ref[...])
    dts = [_softplus(Dt[t][...] + dtb_ref[...]) for t in range(L)]
    cs = []
    for t in range(L):
        dA = dts[t] * A
        cs.append(dA if t == 0 else cs[t - 1] + dA)
    dec24 = jnp.exp(cs[L - 1])
    expcs_e = [_expand_heads(jnp.exp(cs[t]), bt) for t in range(L)]
    xdt = [xs[t] * _expand_heads(dts[t], bt) for t in range(L)]

    ydiag = []
    for t in range(L):
        acc = None
        for s in range(t + 1):
            cb_parts = []
            for g in range(SSD_GROUPS):
                sl = slice(g * D_STATE, (g + 1) * D_STATE)
                r = jnp.sum(Cm[t][:, sl] * Bm[s][:, sl], axis=-1, keepdims=True)
                cb_parts.append(jnp.broadcast_to(r, (bt, GROUP_COLS)))
            coef = jnp.concatenate(cb_parts, axis=1)
            if s < t:
                coef = coef * _expand_heads(jnp.exp(cs[t] - cs[s]), bt)
            term = coef * xdt[s]
            acc = term if acc is None else acc + term
        ydiag.append(acc)

    for t in range(L):
        c_scr[t] = Cm[t]
        b_scr[t] = Bm[t]
        xw_scr[t] = xdt[t] * _expand_heads(jnp.exp(cs[L - 1] - cs[t]), bt)
    zpad_s = jnp.zeros((8 - L, D_STATE), F32)
    zpad_x = jnp.zeros((8 - L, GROUP_COLS), F32)
    for b in range(bt):
        for g in range(SSD_GROUPS):
            sl = slice(g * D_STATE, (g + 1) * D_STATE)
            gs = slice(g * GROUP_COLS, (g + 1) * GROUP_COLS)
            C8 = jnp.concatenate([c_scr[t, b:b + 1, sl] for t in range(L)] + [zpad_s], axis=0)
            B8 = jnp.concatenate([b_scr[t, b:b + 1, sl] for t in range(L)] + [zpad_s], axis=0)
            X8 = jnp.concatenate([xw_scr[t, b:b + 1, gs] for t in range(L)] + [zpad_x], axis=0)
            h0 = sssm_ref[b, SSD_HPG * g:SSD_HPG * (g + 1)].reshape(GROUP_COLS, D_STATE)
            GT = lax.dot_general(C8.astype(BF16), h0.astype(BF16), (((1,), (1,)), ((), ())),
                                 preferred_element_type=F32)
            U = lax.dot_general(X8, B8, (((0,), (0,)), ((), ())), preferred_element_type=F32)
            for r in range(SSD_HPG):
                h = SSD_HPG * g + r
                dec = jnp.broadcast_to(dec24[b:b + 1, h:h + 1], (SSD_HEAD_DIM, D_STATE))
                rs = slice(r * SSD_HEAD_DIM, (r + 1) * SSD_HEAD_DIM)
                nssm_ref[b, h] = h0[rs] * dec + U[rs]
            for t in range(L):
                yoff_scr[t, b:b + 1, gs] = GT[t:t + 1, :]

    for t in range(L):
        y = ydiag[t] + expcs_e[t] * yoff_scr[t] + dskip_ref[...] * xs[t]
        z = P[t][:, POOL_WIDTH:POOL_WIDTH + SSD_WIDTH]
        mix_ref[t, :, POOL_WIDTH:D_MODEL] = _gated_norm(y, z, ng_ref[...])


def _mix_sample(proj, dtp, sconv_t, spool_t, sssm, lw, l, nssm_stack):
    bt = SAMPLE_BT
    nb = DEC_BATCH // bt
    base = TP // bt
    aliased = nssm_stack is not None

    def pspec(t, width):
        return pl.BlockSpec((bt, width), lambda i, t=t: (base + t * nb + i, 0))

    full = lambda shape: pl.BlockSpec(shape, lambda i: (0,) * len(shape))
    return pl.pallas_call(
        functools.partial(_mixs_kernel, aliased=aliased),
        grid=(nb,),
        input_output_aliases={0: 2} if aliased else {},
        in_specs=([pl.BlockSpec(memory_space=pl.ANY)] if aliased else [])
        + [pspec(t, MAIN_WIDTH) for t in range(DEC_SEQ)] + [pspec(t, LANES) for t in range(DEC_SEQ)] + [
            pl.BlockSpec((None, SSD_CONV - 1, bt, XBC_WIDTH), lambda i: (l, 0, i, 0)),
            pl.BlockSpec((None, POOL_BUF, bt, POOL_WIDTH), lambda i: (l, 0, i, 0)),
            pl.BlockSpec((None, bt, SSD_HEADS, SSD_HEAD_DIM, D_STATE), lambda i: (l, i, 0, 0, 0)),
            full((len(POOL_WINDOWS), POOL_GROUP_DIM, POOL_GROUP_DIM)),
            full((1, POOL_WIDTH)),
            full((SSD_CONV, XBC_WIDTH)),
            full((1, XBC_WIDTH)),
            full((1, LANES)), full((1, LANES)),
            full((1, SSD_WIDTH)), full((1, SSD_WIDTH))],
        out_specs=[pl.BlockSpec((DEC_SEQ, bt, D_MODEL), lambda i: (0, i, 0)),
                   pl.BlockSpec((SSD_CONV - 1, bt, XBC_WIDTH), lambda i: (0, i, 0)),
                   pl.BlockSpec((None, bt, SSD_HEADS, SSD_HEAD_DIM, D_STATE), lambda i: (l, i, 0, 0, 0))],
        out_shape=[jax.ShapeDtypeStruct((DEC_SEQ, DEC_BATCH, D_MODEL), F32),
                   jax.ShapeDtypeStruct((SSD_CONV - 1, DEC_BATCH, XBC_WIDTH), F32),
                   jax.ShapeDtypeStruct((DEPTH, DEC_BATCH, SSD_HEADS, SSD_HEAD_DIM, D_STATE), F32)],
        scratch_shapes=[pltpu.VMEM((DEC_SEQ, bt, SSD_GROUPS * D_STATE), F32),
                        pltpu.VMEM((DEC_SEQ, bt, SSD_GROUPS * D_STATE), F32),
                        pltpu.VMEM((DEC_SEQ, bt, SSD_WIDTH), F32),
                        pltpu.VMEM((DEC_SEQ, bt, SSD_WIDTH), F32)],
        compiler_params=_cparams(("arbitrary",)),
        name="mix_sample",
    )(*([nssm_stack] if aliased else []),
      proj, proj, proj, proj, dtp, dtp, dtp, dtp, sconv_t, spool_t, sssm,
      lw["pool_w"], lw["pool_scale"], lw["conv_w"], lw["conv_b"], lw["dt_bias"], lw["a_log"],
      lw["d_skip_e"], lw["ssd_norm"])


RESID_PIECE_ROWS = 256


def _row_pieces(tm):
    return [slice(r0, r0 + RESID_PIECE_ROWS) for r0 in range(0, tm, RESID_PIECE_ROWS)]


def _resid_epilogue(acc, x_ref, r, gp, gs, ng_ref, mods, is_s):
    rows = r.stop - r.start
    xn = x_ref[r, :] + _rowmod(gp, gs, is_s, rows) * acc
    if mods is None:
        ms = jnp.mean(xn * xn, axis=-1, keepdims=True)
        return xn, xn * lax.rsqrt(ms + EPS) * ng_ref[...]
    scp, scs, shp, shs = mods
    return xn, _rms_mod(xn, ng_ref[...], _rowmod(scp, scs, is_s, rows), _rowmod(shp, shs, is_s, rows))


def _outproj_kernel(ap_ref, as_ref, w_ref, x_ref, gp, gs, ng_ref, scp, scs, shp, shs, xo_ref, ho_ref, *, npt, tm):
    is_s = pl.program_id(0) >= npt
    for r in _row_pieces(tm):
        a = jnp.where(is_s, as_ref[r, :].astype(BF16), ap_ref[r, :])
        acc = jnp.dot(a, w_ref[...], preferred_element_type=F32)
        xn, hn = _resid_epilogue(acc, x_ref, r, gp, gs, ng_ref, (scp, scs, shp, shs), is_s)
        xo_ref[r, :] = xn
        ho_ref[r, :] = hn.astype(ho_ref.dtype)


def _down_kernel(a_ref, w_ref, x_ref, gp, gs, ng_ref, scp, scs, shp, shs, xo_ref, ho_ref, *, npt, tm):
    is_s = pl.program_id(0) >= npt
    for r in _row_pieces(tm):
        acc = jnp.dot(a_ref[r, :], w_ref[...], preferred_element_type=F32)
        xn, hn = _resid_epilogue(acc, x_ref, r, gp, gs, ng_ref, (scp, scs, shp, shs), is_s)
        xo_ref[r, :] = xn
        ho_ref[r, :] = hn.astype(ho_ref.dtype)


def _final_kernel(a_ref, w_ref, x_ref, gp, gs, ng_ref, yp_ref, ys_ref, *, npt, tm):
    is_s = pl.program_id(0) >= npt
    acc = jnp.dot(a_ref[...], w_ref[...], preferred_element_type=F32)
    _, yn = _resid_epilogue(acc, x_ref, slice(0, tm), gp, gs, ng_ref, None, is_s)

    @pl.when(jnp.logical_not(is_s))
    def _():
        yp_ref[...] = yn

    @pl.when(is_s)
    def _():
        ys_ref[...] = yn


def _resid_proj(a_list, w, x, gain, modp, mods, l, gate_k, next_mod):
    kdim = w.shape[-2]
    tm = 512 if kdim <= D_MODEL else 256
    npt = TP // tm
    row = lambda i: i
    if w.ndim == 3:
        w_spec = pl.BlockSpec((None, kdim, D_MODEL), lambda i: (l, 0, 0), pipeline_mode=pl.Buffered(1))
    else:
        w_spec = pl.BlockSpec((kdim, D_MODEL), lambda i: (0, 0), pipeline_mode=pl.Buffered(1))
    gp, gs = _mod_specs(l, gate_k, tm, row)
    if len(a_list) == 2:
        in_specs = [pl.BlockSpec((tm, kdim), lambda i: (jnp.minimum(i, npt - 1), 0)),
                    pl.BlockSpec((tm, kdim), lambda i: (jnp.maximum(i - npt, 0), 0))]
        kern = _outproj_kernel
    else:
        in_specs = [pl.BlockSpec((tm, kdim), lambda i: (i, 0))]
        kern = _down_kernel
    in_specs += [w_spec,
                 pl.BlockSpec((tm, D_MODEL), lambda i: (i, 0)),
                 gp, gs,
                 pl.BlockSpec((1, D_MODEL), lambda i: (0, 0))]
    args = list(a_list) + [w, x, modp, mods, gain.reshape(1, D_MODEL)]
    if next_mod is None:
        kern = _final_kernel
        out_specs = [pl.BlockSpec((tm, D_MODEL), lambda i: (jnp.minimum(i, npt - 1), 0)),
                     pl.BlockSpec((tm, D_MODEL), lambda i: (jnp.maximum(i - npt, 0), 0))]
        out_shape = [jax.ShapeDtypeStruct((TP, D_MODEL), F32), jax.ShapeDtypeStruct((TS, D_MODEL), F32)]
    else:
        nl, shift_k, scale_k = next_mod
        scp, scs = _mod_specs(nl, scale_k, tm, row)
        shp, shs = _mod_specs(nl, shift_k, tm, row)
        in_specs += [scp, scs, shp, shs]
        args += [modp, mods, modp, mods]
        out_specs = [pl.BlockSpec((tm, D_MODEL), lambda i: (i, 0)),
                     pl.BlockSpec((tm, D_MODEL), lambda i: (i, 0))]
        out_shape = [jax.ShapeDtypeStruct((T, D_MODEL), F32), jax.ShapeDtypeStruct((T, D_MODEL), BF16)]
    return pl.pallas_call(
        functools.partial(kern, npt=npt, tm=tm),
        grid=(T // tm,),
        in_specs=in_specs,
        out_specs=out_specs,
        out_shape=out_shape,
        compiler_params=_cparams(("arbitrary",)),
        name="resid_proj",
    )(*args)


def _up_kernel(h_ref, wg_ref, wv0, wv1, wv2, wv3, cw_ref, cb_ref, sf_ref, wd_ref,
               act_ref, nfp_ref, nfs_ref, wdb_ref, wgb, wvb, gpad, *, npt, tpb, tm, tn):
    m = pl.program_id(1)

    @pl.when(m == 0)
    def _():
        wgb[...] = wg_ref[...].astype(BF16)
        for k, r in enumerate((wv0, wv1, wv2, wv3)):
            wvb[:, k * LANES:(k + 1) * LANES] = r[...].astype(BF16)
        wdb_ref[...] = wd_ref[...].astype(BF16)

    is_s = m >= npt
    first = (m % tpb) == 0
    hw = tn // 2
    B1, B2 = DEC_BATCH, 2 * DEC_BATCH
    plan = [(0, tm // 2, 0), (tm // 2, tm // 2, 0),
            (0, tm // 2, 1), (tm // 2, tm // 4, 1), (3 * tm // 4, tm // 4, 1)]
    h_halves = (h_ref[0:tm // 2, :], h_ref[tm // 2:tm, :])
    for r0, rows, half in plan:
        cs = slice(half * hw, (half + 1) * hw)
        hc = h_halves[r0 // (tm // 2)]
        if rows < tm // 2:
            q0 = r0 % (tm // 2)
            hc = hc[q0:q0 + rows]
        gate = jnp.dot(hc, wgb[:, cs], preferred_element_type=F32)
        val = jnp.dot(hc, wvb[:, cs], preferred_element_type=F32)
        if r0 == 0:
            gpad[0:8, cs] = jnp.where(first, 0.0, gpad[0:8, cs])
        gpad[8 + r0:8 + r0 + rows, cs] = gate
        gm1 = gpad[7 + r0:7 + r0 + rows, cs]
        gm2 = gpad[6 + r0:6 + r0 + rows, cs]
        if r0 < TS:
            if r0 == 0:
                s1 = jnp.concatenate([sf_ref[1, :, cs], gpad[8:8 + rows - B1, cs]], axis=0)
                s2 = jnp.concatenate([sf_ref[0, :, cs], sf_ref[1, :, cs], gpad[8:8 + rows - B2, cs]], axis=0)
            else:
                s1 = gpad[8 + r0 - B1:8 + r0 - B1 + rows, cs]
                s2 = gpad[8 + r0 - B2:8 + r0 - B2 + rows, cs]
            gm1 = jnp.where(is_s, s1, gm1)
            gm2 = jnp.where(is_s, s2, gm2)
        gc = cb_ref[:, cs] + cw_ref[2:3, cs] * gate + cw_ref[1:2, cs] * gm1 + cw_ref[0:1, cs] * gm2
        act_ref[r0:r0 + rows, cs] = (_silu(gc) * val).astype(act_ref.dtype)
        if r0 + rows == tm:
            gpad[0:8, cs] = gpad[tm:tm + 8, cs]
            nfp_ref[:, cs] = jnp.where(is_s, 0.0, gate[rows - (FFN_CONV - 1):rows])
        if r0 == 0:
            nfs_ref[0, :, cs] = gate[(DEC_SEQ - 2) * B1:(DEC_SEQ - 1) * B1]
            nfs_ref[1, :, cs] = gate[(DEC_SEQ - 1) * B1:DEC_SEQ * B1]


def _ffn_up(h2, w_up, conv_w, conv_b, sffn_t, w_down, l):
    tm, tn = 1024, 512
    assert TS % (tm // 4) == 0 and TS <= tm // 2 and SEQ % tm == 0
    npt = TP // tm
    tpb = SEQ // tm
    nn = pl.cdiv(D_FF, tn)
    val0 = D_FF // LANES
    last = 2 * D_FF // LANES - 1

    def vspec(k):
        return pl.BlockSpec((None, D_MODEL, LANES),
                            lambda j, m: (l, 0, jnp.minimum(val0 + (tn // LANES) * j + k, last)))

    return pl.pallas_call(
        functools.partial(_up_kernel, npt=npt, tpb=tpb, tm=tm, tn=tn),
        grid=(nn, pl.cdiv(T, tm)),
        in_specs=[pl.BlockSpec((tm, D_MODEL), lambda j, m: (m, 0)),
                  pl.BlockSpec((None, D_MODEL, tn), lambda j, m: (l, 0, j))]
                 + [vspec(k) for k in range(tn // LANES)]
                 + [pl.BlockSpec((FFN_CONV, tn), lambda j, m: (0, j)),
                    pl.BlockSpec((1, tn), lambda j, m: (0, j)),
                    pl.BlockSpec((None, FFN_CONV - 1, DEC_BATCH, tn), lambda j, m: (l, 0, 0, j)),
                    pl.BlockSpec((None, tn, D_MODEL), lambda j, m: (l, j, 0))],
        out_specs=[pl.BlockSpec((tm, tn), lambda j, m: (m, j)),
                   pl.BlockSpec((None, FFN_CONV - 1, tn), lambda j, m: (jnp.minimum(m // tpb, BATCH), 0, j)),
                   pl.BlockSpec((FFN_CONV - 1, DEC_BATCH, tn), lambda j, m: (0, 0, j)),
                   pl.BlockSpec((tn, D_MODEL), lambda j, m: (j, 0))],
        out_shape=[jax.ShapeDtypeStruct((T, D_FF), BF16),
                   jax.ShapeDtypeStruct((BATCH + 1, FFN_CONV - 1, D_FF), F32),
                   jax.ShapeDtypeStruct((FFN_CONV - 1, DEC_BATCH, D_FF), F32),
                   jax.ShapeDtypeStruct((D_FF, D_MODEL), BF16)],
        scratch_shapes=[pltpu.VMEM((D_MODEL, tn), BF16), pltpu.VMEM((D_MODEL, tn), BF16),
                        pltpu.VMEM((8 + tm, tn), F32)],
        compiler_params=_cparams(("arbitrary", "arbitrary")),
        name="ffn_up",
    )(h2, w_up, w_up, w_up, w_up, w_up, conv_w, conv_b.reshape(1, D_FF), sffn_t, w_down)


def _pad_lanes(v):
    return jnp.pad(v, ((0, 0), (0, LANES - v.shape[-1])))


def kernel(x_prompt, x_sample, c_prompt, c_sample, state_pool, state_conv, state_ssm, state_ffn, norm1, norm2, w_ada, b_ada, w_in, pool_w, pool_scale, conv_w, conv_b, dt_bias, a_log, d_skip, ssd_norm, w_out, w_up, ffn_conv_w, ffn_conv_b, w_down, norm_f):
    x = jnp.concatenate([x_prompt.reshape(TP, D_MODEL),
                         jnp.swapaxes(x_sample, 0, 1).reshape(TS, D_MODEL)], axis=0)
    c_all = jnp.concatenate([c_sample, c_prompt, jnp.zeros((4, D_MODEL), F32)], axis=0)
    mods, modp = _ada(c_all, w_ada, b_ada)
    modp = modp.reshape(DEPTH, BATCH, 1, 6 * D_MODEL)

    spool_t = jnp.swapaxes(state_pool, 1, 2)
    sconv_t = jnp.swapaxes(state_conv, 1, 2)
    sffn_t = jnp.swapaxes(state_ffn, 1, 2)

    w_in_t = jnp.swapaxes(w_in, 1, 2)

    d_skip_e = jnp.repeat(d_skip, SSD_HEAD_DIM, axis=-1)

    outs = {k: [] for k in ("pp", "pc", "ps", "pf", "sp", "sc", "sf")}
    h = _norm0(x, norm1[0], modp, mods, 0)
    nssm_stack = None
    for l in range(DEPTH):
        lw = {"pool_w": pool_w[l], "pool_scale": pool_scale[l].reshape(1, POOL_WIDTH),
              "conv_w": conv_w[l], "conv_b": conv_b[l].reshape(1, XBC_WIDTH),
              "dt_bias": _pad_lanes(dt_bias[l].reshape(1, SSD_HEADS)),
              "a_log": _pad_lanes(a_log[l].reshape(1, SSD_HEADS)),
              "d_skip_e": d_skip_e[l].reshape(1, SSD_WIDTH),
              "ssd_norm": ssd_norm[l].reshape(1, SSD_WIDTH)}
        proj, dtp = _inproj(h, w_in_t, l)
        mix_p, st_p, w_out_b = _mix_prompt(proj, dtp, lw, w_out, l)
        mix_s, nconv_s, nssm_stack = _mix_sample(proj, dtp, sconv_t, spool_t, state_ssm, lw, l, nssm_stack)
        x1, h2 = _resid_proj([mix_p, mix_s.reshape(TS, D_MODEL)], w_out_b, x, norm2[l], modp, mods, l, 2, (l, 3, 4))
        act, nffn_p, nffn_s, w_down_b = _ffn_up(h2, w_up, ffn_conv_w[l], ffn_conv_b[l], sffn_t, w_down, l)
        if l + 1 < DEPTH:
            x, h = _resid_proj([act], w_down_b, x1, norm1[l + 1], modp, mods, l, 5, (l + 1, 0, 1))
        else:
            y_p, y_s = _resid_proj([act], w_down_b, x1, norm_f, modp, mods, l, 5, None)

        ends = [(b + 1) * SEQ for b in range(BATCH)]
        outs["pp"].append(jnp.stack([lax.slice(proj, (e - POOL_BUF, 0), (e, POOL_WIDTH)) for e in ends]))
        outs["pc"].append(jnp.stack([lax.slice(proj, (e - (SSD_CONV - 1), POOL_WIDTH + SSD_WIDTH), (e, MAIN_WIDTH))
                                     for e in ends]))
        outs["ps"].append(jnp.transpose(st_p.reshape(BATCH, D_STATE, SSD_HEADS, SSD_HEAD_DIM), (0, 2, 3, 1)))
        outs["pf"].append(nffn_p[:BATCH])
        u_s = jnp.swapaxes(lax.slice(proj, (TP, 0), (T, POOL_WIDTH)).reshape(DEC_SEQ, DEC_BATCH, POOL_WIDTH), 0, 1)
        outs["sp"].append(jnp.concatenate([state_pool[l][:, DEC_SEQ:], u_s], axis=1))
        outs["sc"].append(jnp.swapaxes(nconv_s, 0, 1))
        outs["sf"].append(jnp.swapaxes(nffn_s, 0, 1))

    y_prompt = y_p.reshape(BATCH, SEQ, D_MODEL)
    y_sample = jnp.swapaxes(y_s.reshape(DEC_SEQ, DEC_BATCH, D_MODEL), 0, 1)
    return (y_prompt, y_sample,
            jnp.stack(outs["pp"]), jnp.stack(outs["pc"]), jnp.stack(outs["ps"]), jnp.stack(outs["pf"]),
            jnp.stack(outs["sp"]), jnp.stack(outs["sc"]), nssm_stack, jnp.stack(outs["sf"]))
```

```python
import functools

import jax
import jax.numpy as jnp
from jax import lax
from jax.experimental import pallas as pl
from jax.experimental.pallas import tpu as pltpu

F32 = jnp.float32
BF16 = jnp.bfloat16

D_MODEL = 2048
BATCH = 4
SEQ = 2048
DEPTH = 4
DEC_BATCH = 128
DEC_SEQ = 4
PAST_LEN = 16384

POOL_WIDTH = 512
POOL_WINDOWS = (2, 4, 8, 16)
POOL_GROUP_DIM = 128
POOL_BUF = 15
SSD_WIDTH = 1536
SSD_HEAD_DIM = 64
SSD_HEADS = 24
SSD_GROUPS = 4
SSD_HPG = 6
D_STATE = 128
SSD_CONV = 4
XBC_WIDTH = 2560
MAIN_WIDTH = POOL_WIDTH + SSD_WIDTH + XBC_WIDTH
D_FF = 5504
FFN_CONV = 3
EPS = 1e-6

TP = BATCH * SEQ
TS = DEC_SEQ * DEC_BATCH
T = TP + TS
GROUP_COLS = SSD_HPG * SSD_HEAD_DIM
LANES = 128
CHUNK = 128

VMEM_LIMIT = 56 * 1024 * 1024


def _silu(x):
    return x * jax.nn.sigmoid(x)


def _softplus(x):
    return jnp.maximum(x, 0.0) + jnp.log1p(jnp.exp(-jnp.abs(x)))


def _cparams(sem):
    return pltpu.CompilerParams(dimension_semantics=sem, vmem_limit_bytes=VMEM_LIMIT)


def _rowmod(p_ref, s_ref, is_sample, tm):
    s = s_ref[...]
    if tm > DEC_BATCH:
        s = jnp.concatenate([s] * (tm // DEC_BATCH), axis=0)
    return jnp.where(is_sample, s, p_ref[...])


def _rms_mod(x, gain, sc, sh):
    ms = jnp.mean(x * x, axis=-1, keepdims=True)
    y = x * lax.rsqrt(ms + EPS) * gain
    return y * (1.0 + sc) + sh


def _ada_kernel(c_ref, w_ref, b_ref, os_ref, op_ref):
    a = _silu(c_ref[...]).astype(BF16)
    m = jnp.dot(a, w_ref[...].astype(BF16), preferred_element_type=F32) + b_ref[...]
    os_ref[...] = m[:DEC_BATCH]
    op_ref[...] = m[DEC_BATCH:DEC_BATCH + BATCH]


def _ada(c_all, w_ada, b_ada):
    tn = 1024
    rows = c_all.shape[0]
    return pl.pallas_call(
        _ada_kernel,
        grid=(DEPTH, 6 * D_MODEL // tn),
        in_specs=[
            pl.BlockSpec((rows, D_MODEL), lambda l, j: (0, 0)),
            pl.BlockSpec((None, D_MODEL, tn), lambda l, j: (l, 0, j)),
            pl.BlockSpec((None, 1, tn), lambda l, j: (l, 0, j)),
        ],
        out_specs=[
            pl.BlockSpec((None, DEC_BATCH, tn), lambda l, j: (l, 0, j)),
            pl.BlockSpec((None, BATCH, tn), lambda l, j: (l, 0, j)),
        ],
        out_shape=[
            jax.ShapeDtypeStruct((DEPTH, DEC_BATCH, 6 * D_MODEL), F32),
            jax.ShapeDtypeStruct((DEPTH, BATCH, 6 * D_MODEL), F32),
        ],
        compiler_params=_cparams(("arbitrary", "arbitrary")),
        name="ada",
    )(c_all, w_ada, b_ada.reshape(DEPTH, 1, 6 * D_MODEL))


def _mod_specs(l, k, tm, row_of):
    tpb = SEQ // tm

    def pmap(*ids):
        return (l, jnp.minimum(row_of(*ids) // tpb, BATCH - 1), 0, k)

    def smap(*ids):
        return (l, 0, k)

    return (pl.BlockSpec((None, None, 1, D_MODEL), pmap), pl.BlockSpec((None, DEC_BATCH, D_MODEL), smap))


def _norm_kernel(x_ref, g_ref, scp, scs, shp, shs, o_ref, *, npt, tm):
    is_s = pl.program_id(0) >= npt
    sc = _rowmod(scp, scs, is_s, tm)
    sh = _rowmod(shp, shs, is_s, tm)
    o_ref[...] = _rms_mod(x_ref[...], g_ref[...], sc, sh).astype(o_ref.dtype)


def _norm0(x, gain, modp, mods, l):
    tm = 256
    npt = TP // tm
    row = lambda i: i
    scp, scs = _mod_specs(l, 1, tm, row)
    shp, shs = _mod_specs(l, 0, tm, row)
    return pl.pallas_call(
        functools.partial(_norm_kernel, npt=npt, tm=tm),
        grid=(T // tm,),
        in_specs=[pl.BlockSpec((tm, D_MODEL), lambda i: (i, 0)),
                  pl.BlockSpec((1, D_MODEL), lambda i: (0, 0)),
                  scp, scs, shp, shs],
        out_specs=pl.BlockSpec((tm, D_MODEL), lambda i: (i, 0)),
        out_shape=jax.ShapeDtypeStruct((T, D_MODEL), BF16),
        compiler_params=_cparams(("arbitrary",)),
        name="norm0",
    )(x, gain.reshape(1, D_MODEL), modp, mods, modp, mods)


_NT = (((1,), (1,)), ((), ()))


def _inproj_kernel(h_ref, w_ref, wdt_ref, o_ref, dt_ref, wb, wdtb):
    j = pl.program_id(0)
    m = pl.program_id(1)

    @pl.when(m == 0)
    def _():
        wb[...] = w_ref[...].astype(BF16)

    h = h_ref[...]
    o_ref[...] = lax.dot_general(h, wb[...], _NT, preferred_element_type=F32)

    @pl.when(j == 0)
    def _():
        @pl.when(m == 0)
        def _():
            row = lax.broadcasted_iota(jnp.int32, (LANES, 1), 0)
            wdtb[...] = jnp.where(row < SSD_HEADS, wdt_ref[...], 0.0).astype(BF16)

        dt_ref[...] = lax.dot_general(h, wdtb[...], _NT, preferred_element_type=F32)

    @pl.when(j != 0)
    def _():
        dt_ref[...] = jnp.zeros(dt_ref.shape, F32)


def _inproj(h, w_in_t, l):
    tm, tn = 512, 1536
    nm = T // tm
    return pl.pallas_call(
        _inproj_kernel,
        grid=(MAIN_WIDTH // tn, nm),
        in_specs=[pl.BlockSpec((tm, D_MODEL), lambda j, m: (m, 0)),
                  pl.BlockSpec((None, tn, D_MODEL), lambda j, m: (l, j, 0)),
                  pl.BlockSpec((None, LANES, D_MODEL), lambda j, m: (l, MAIN_WIDTH // LANES, 0))],
        out_specs=[pl.BlockSpec((tm, tn), lambda j, m: (m, j)),
                   pl.BlockSpec((tm, LANES), lambda j, m: (jnp.where(j == 0, m, nm), 0))],
        out_shape=[jax.ShapeDtypeStruct((T, MAIN_WIDTH), F32),
                   jax.ShapeDtypeStruct((T + tm, LANES), F32)],
        scratch_shapes=[pltpu.VMEM((tn, D_MODEL), BF16), pltpu.VMEM((LANES, D_MODEL), BF16)],
        compiler_params=_cparams(("arbitrary", "arbitrary")),
        name="inproj",
    )(h, w_in_t, w_in_t)


def _expand_heads(v, rows):
    lo = lax.broadcasted_iota(jnp.int32, (rows, LANES), 1) < SSD_HEAD_DIM
    parts = []
    for j in range(SSD_HEADS // 2):
        a = jnp.broadcast_to(v[:, 2 * j:2 * j + 1], (rows, LANES))
        b = jnp.broadcast_to(v[:, 2 * j + 1:2 * j + 2], (rows, LANES))
        parts.append(jnp.where(lo, a, b))
    return jnp.concatenate(parts, axis=1)


def _split3(x):
    a = x.astype(BF16)
    r = x - a.astype(F32)
    b = r.astype(BF16)
    c = (r - b.astype(F32)).astype(BF16)
    return a, b, c


def _gated_norm(y, z, gain):
    y = y * _silu(z)
    ms = jnp.mean(y * y, axis=-1, keepdims=True)
    return y * lax.rsqrt(ms + EPS) * gain


PROMPT_CHUNKS_PER_STEP = 2


def _mixp_kernel(proj_ref, dt_ref, poolw_ref, pscale_ref, cw_ref, cb_ref, dtb_ref, alog_ref, dskip_ref, ng_ref, hc_ref,
                 wo_ref, mix_ref, st_ref, wob_ref, xpad, upad, hT):
    c = pl.program_id(1)

    @pl.when(c == 0)
    def _():
        xpad[0:8, :] = jnp.zeros((8, XBC_WIDTH), F32)
        upad[0:16, :] = jnp.zeros((16, POOL_WIDTH), F32)
        hT[...] = jnp.zeros(hT.shape, F32)

    wob_ref[...] = wo_ref[...].astype(BF16)

    for sub in range(PROMPT_CHUNKS_PER_STEP):
        _mixp_chunk(c * PROMPT_CHUNKS_PER_STEP + sub, slice(sub * CHUNK, (sub + 1) * CHUNK),
                    proj_ref, dt_ref, poolw_ref, pscale_ref, cw_ref, cb_ref, dtb_ref, alog_ref, dskip_ref, ng_ref,
                    hc_ref, mix_ref, xpad, upad, hT)

    @pl.when(c == pl.num_programs(1) - 1)
    def _():
        st_ref[...] = hT[...]


def _mixp_chunk(chunk, r, proj_ref, dt_ref, poolw_ref, pscale_ref, cw_ref, cb_ref, dtb_ref, alog_ref, dskip_ref, ng_ref,
                hc_ref, mix_ref, xpad, upad, hT):
    Q = CHUNK

    u = proj_ref[r, 0:POOL_WIDTH]
    upad[16:16 + Q, :] = u
    pos = chunk * Q + lax.broadcasted_iota(jnp.int32, (Q, 1), 0)
    outs = []
    for g, w in enumerate(POOL_WINDOWS):
        lo = g * POOL_GROUP_DIM
        ug = u[:, lo:lo + POOL_GROUP_DIM]
        s = ug
        for k in range(1, w):
            s = s + upad[16 - k:16 - k + Q, lo:lo + POOL_GROUP_DIM]
        cnt = jnp.minimum(pos + 1, w).astype(F32)
        d = s / cnt - ug
        outs.append(jnp.dot(d.astype(BF16), poolw_ref[g].astype(BF16), preferred_element_type=F32))
    mix_ref[r, 0:POOL_WIDTH] = (jnp.concatenate(outs, axis=1) * pscale_ref[...]).astype(mix_ref.dtype)
    upad[0:16, :] = upad[Q:Q + 16, :]

    xbc = proj_ref[r, POOL_WIDTH + SSD_WIDTH:MAIN_WIDTH]
    xpad[8:8 + Q, :] = xbc
    acc = cb_ref[...] + cw_ref[3:4, :] * xbc
    for i in range(SSD_CONV - 1):
        acc = acc + cw_ref[i:i + 1, :] * xpad[5 + i:5 + i + Q, :]
    xpad[0:8, :] = xpad[Q:Q + 8, :]
    xc = _silu(acc)
    xs = xc[:, 0:SSD_WIDTH]
    Bb = xc[:, SSD_WIDTH:SSD_WIDTH + SSD_GROUPS * D_STATE].astype(BF16)
    Cb = xc[:, SSD_WIDTH + SSD_GROUPS * D_STATE:XBC_WIDTH].astype(BF16)

    dt = _softplus(dt_ref[r, :] + dtb_ref[...])
    dA = dt * (-jnp.exp(alog_ref[...]))
    row = lax.broadcasted_iota(jnp.int32, (Q, Q), 0)
    col = lax.broadcasted_iota(jnp.int32, (Q, Q), 1)
    tri = row >= col
    tri_lo = jnp.where(tri, 1.0, 0.0).astype(BF16)
    tri_up = jnp.where(row <= col, 1.0, 0.0).astype(BF16)
    pieces = _split3(dA)
    cs = sum(jnp.dot(tri_lo, p, preferred_element_type=F32) for p in pieces)
    csT = sum(lax.dot_general(p, tri_up, (((0,), (0,)), ((), ())), preferred_element_type=F32)
              for p in pieces)
    cs_last = cs[Q - 1:Q, :]
    stacked = jnp.concatenate([dt, jnp.exp(cs), jnp.exp(cs_last - cs)], axis=0)
    head_cols = hc_ref[...]
    expanded = sum(jnp.dot(p, head_cols, preferred_element_type=F32) for p in _split3(stacked))
    expcs_e = expanded[Q:2 * Q]
    dte_e = expanded[2 * Q:3 * Q]
    xdt = xs * expanded[0:Q]
    xdt_b = xdt.astype(BF16)
    lo_half = lax.broadcasted_iota(jnp.int32, (Q, LANES), 1) < SSD_HEAD_DIM
    xw_b = (xdt * dte_e).astype(BF16)
    decay_e = expcs_e[Q - 1:Q, :]
    h_prev = hT[...]
    h_prev_b = h_prev.astype(BF16)

    ys = []
    for g in range(SSD_GROUPS):
        gs = slice(g * GROUP_COLS, (g + 1) * GROUP_COLS)
        Cg = Cb[:, g * D_STATE:(g + 1) * D_STATE]
        Bg = Bb[:, g * D_STATE:(g + 1) * D_STATE]
        CB = lax.dot_general(Cg, Bg, (((1,), (1,)), ((), ())), preferred_element_type=F32)
        y_off = jnp.dot(Cg, h_prev_b[:, gs], preferred_element_type=F32)
        ST = lax.dot_general(Bg, xw_b[:, gs], (((0,), (0,)), ((), ())), preferred_element_type=F32)
        hT[:, gs] = h_prev[:, gs] * decay_e[:, gs] + ST
        parts = []
        for j in range(SSD_HPG // 2):
            h1 = SSD_HPG * g + 2 * j
            ps = slice(h1 * SSD_HEAD_DIM, h1 * SSD_HEAD_DIM + LANES)
            W = []
            for h in (h1, h1 + 1):
                seg = cs[:, h:h + 1] - csT[h:h + 1, :]
                W.append((CB * jnp.exp(jnp.where(tri, seg, -jnp.inf))).astype(BF16))
            parts.append(jnp.where(lo_half,
                                   jnp.dot(W[0], xdt_b[:, ps], preferred_element_type=F32),
                                   jnp.dot(W[1], xdt_b[:, ps], preferred_element_type=F32)))
        ys.append(jnp.concatenate(parts, axis=1) + y_off * expcs_e[:, gs])
    y = jnp.concatenate(ys, axis=1) + dskip_ref[...] * xs
    z = proj_ref[r, POOL_WIDTH:POOL_WIDTH + SSD_WIDTH]
    mix_ref[r, POOL_WIDTH:D_MODEL] = _gated_norm(y, z, ng_ref[...]).astype(mix_ref.dtype)


def _mix_prompt(proj, dtp, lw, w_out, l):
    rows = CHUNK * PROMPT_CHUNKS_PER_STEP
    nc = SEQ // rows
    wrows = D_MODEL // (BATCH * nc)
    full = lambda shape: pl.BlockSpec(shape, lambda b, c: (0,) * len(shape))
    return pl.pallas_call(
        _mixp_kernel,
        grid=(BATCH, nc),
        in_specs=[pl.BlockSpec((rows, MAIN_WIDTH), lambda b, c: (b * nc + c, 0)),
                  pl.BlockSpec((rows, LANES), lambda b, c: (b * nc + c, 0)),
                  full((len(POOL_WINDOWS), POOL_GROUP_DIM, POOL_GROUP_DIM)),
                  full((1, POOL_WIDTH)),
                  full((SSD_CONV, XBC_WIDTH)),
                  full((1, XBC_WIDTH)),
                  full((1, LANES)), full((1, LANES)),
                  full((1, SSD_WIDTH)), full((1, SSD_WIDTH)),
                  full((LANES, SSD_WIDTH)),
                  pl.BlockSpec((None, wrows, D_MODEL), lambda b, c: (l, b * nc + c, 0))],
        out_specs=[pl.BlockSpec((rows, D_MODEL), lambda b, c: (b * nc + c, 0)),
                   pl.BlockSpec((None, D_STATE, SSD_WIDTH), lambda b, c: (b, 0, 0)),
                   pl.BlockSpec((wrows, D_MODEL), lambda b, c: (b * nc + c, 0))],
        out_shape=[jax.ShapeDtypeStruct((TP, D_MODEL), BF16),
                   jax.ShapeDtypeStruct((BATCH, D_STATE, SSD_WIDTH), F32),
                   jax.ShapeDtypeStruct((D_MODEL, D_MODEL), BF16)],
        scratch_shapes=[pltpu.VMEM((8 + CHUNK, XBC_WIDTH), F32),
                        pltpu.VMEM((16 + CHUNK, POOL_WIDTH), F32),
                        pltpu.VMEM((D_STATE, SSD_WIDTH), F32)],
        compiler_params=_cparams(("arbitrary", "arbitrary")),
        name="mix_prompt",
    )(proj, dtp, lw["pool_w"], lw["pool_scale"], lw["conv_w"], lw["conv_b"], lw["dt_bias"], lw["a_log"],
      lw["d_skip_e"], lw["ssd_norm"], _head_cols(), w_out)


def _head_cols():
    head_of_col = jnp.arange(SSD_WIDTH, dtype=jnp.int32) // SSD_HEAD_DIM
    return (head_of_col[None, :] == jnp.arange(LANES, dtype=jnp.int32)[:, None]).astype(BF16)


SAMPLE_BT = 8


def _mixs_kernel(*refs, aliased):
    if aliased:
        refs = refs[1:]
    (p0, p1, p2, p3, d0, d1, d2, d3, sconv_ref, spool_ref, sssm_ref,
     poolw_ref, pscale_ref, cw_ref, cb_ref, dtb_ref, alog_ref, dskip_ref, ng_ref,
     mix_ref, nconv_ref, nssm_ref, c_scr, b_scr, xw_scr, yoff_scr) = refs
    bt = SAMPLE_BT
    P = (p0, p1, p2, p3)
    Dt = (d0, d1, d2, d3)
    L = DEC_SEQ

    fullp = [spool_ref[k] for k in range(POOL_BUF)] + [P[t][:, 0:POOL_WIDTH] for t in range(L)]
    for g, w in enumerate(POOL_WINDOWS):
        gl = slice(g * POOL_GROUP_DIM, (g + 1) * POOL_GROUP_DIM)
        ds = []
        for t in range(L):
            s = fullp[POOL_BUF + t][:, gl]
            for k in range(1, w):
                s = s + fullp[POOL_BUF + t - k][:, gl]
            cnt = float(min(PAST_LEN + t + 1, w))
            ds.append(s / cnt - fullp[POOL_BUF + t][:, gl])
        o = jnp.dot(jnp.concatenate(ds, axis=0).astype(BF16), poolw_ref[g].astype(BF16),
                    preferred_element_type=F32) * pscale_ref[:, gl]
        for t in range(L):
            mix_ref[t, :, gl] = o[t * bt:(t + 1) * bt]

    fullc = [sconv_ref[k] for k in range(SSD_CONV - 1)] + [P[t][:, POOL_WIDTH + SSD_WIDTH:MAIN_WIDTH] for t in range(L)]
    for k in range(SSD_CONV - 1):
        nconv_ref[k] = fullc[L + k]
    xs, Bm, Cm = [], [], []
    for t in range(L):
        acc = cb_ref[...]
        for i in range(SSD_CONV):
            acc = acc + cw_ref[i:i + 1, :] * fullc[t + i]
        xc = _silu(acc)
        xs.append(xc[:, 0:SSD_WIDTH])
        Bm.append(xc[:, SSD_WIDTH:SSD_WIDTH + SSD_GROUPS * D_STATE])
        Cm.append(xc[:, SSD_WIDTH + SSD_GROUPS * D_STATE:XBC_WIDTH])

    A = -jnp.exp(alog_ref[...])
    dts = [_softplus(Dt[t][...] + dtb_ref[...]) for t in range(L)]
    cs = []
    for t in range(L):
        dA = dts[t] * A
        cs.append(dA if t == 0 else cs[t - 1] + dA)
    dec24 = jnp.exp(cs[L - 1])
    expcs_e = [_expand_heads(jnp.exp(cs[t]), bt) for t in range(L)]
    xdt = [xs[t] * _expand_heads(dts[t], bt) for t in range(L)]

    ydiag = []
    for t in range(L):
        acc = None
        for s in range(t + 1):
            cb_parts = []
            for g in range(SSD_GROUPS):
                sl = slice(g * D_STATE, (g + 1) * D_STATE)
                r = jnp.sum(Cm[t][:, sl] * Bm[s][:, sl], axis=-1, keepdims=True)
                cb_parts.append(jnp.broadcast_to(r, (bt, GROUP_COLS)))
            coef = jnp.concatenate(cb_parts, axis=1)
            if s < t:
                coef = coef * _expand_heads(jnp.exp(cs[t] - cs[s]), bt)
            term = coef * xdt[s]
            acc = term if acc is None else acc + term
        ydiag.append(acc)

    for t in range(L):
        c_scr[t] = Cm[t]
        b_scr[t] = Bm[t]
        xw_scr[t] = xdt[t] * _expand_heads(jnp.exp(cs[L - 1] - cs[t]), bt)
    zpad_s = jnp.zeros((8 - L, D_STATE), F32)
    zpad_x = jnp.zeros((8 - L, GROUP_COLS), F32)
    for b in range(bt):
        for g in range(SSD_GROUPS):
            sl = slice(g * D_STATE, (g + 1) * D_STATE)
            gs = slice(g * GROUP_COLS, (g + 1) * GROUP_COLS)
            C8 = jnp.concatenate([c_scr[t, b:b + 1, sl] for t in range(L)] + [zpad_s], axis=0)
            B8 = jnp.concatenate([b_scr[t, b:b + 1, sl] for t in range(L)] + [zpad_s], axis=0)
            X8 = jnp.concatenate([xw_scr[t, b:b + 1, gs] for t in range(L)] + [zpad_x], axis=0)
            h0 = sssm_ref[b, SSD_HPG * g:SSD_HPG * (g + 1)].reshape(GROUP_COLS, D_STATE)
            GT = lax.dot_general(C8.astype(BF16), h0.astype(BF16), (((1,), (1,)), ((), ())),
                                 preferred_element_type=F32)
            U = lax.dot_general(X8, B8, (((0,), (0,)), ((), ())), preferred_element_type=F32)
            for r in range(SSD_HPG):
                h = SSD_HPG * g + r
                dec = jnp.broadcast_to(dec24[b:b + 1, h:h + 1], (SSD_HEAD_DIM, D_STATE))
                rs = slice(r * SSD_HEAD_DIM, (r + 1) * SSD_HEAD_DIM)
                nssm_ref[b, h] = h0[rs] * dec + U[rs]
            for t in range(L):
                yoff_scr[t, b:b + 1, gs] = GT[t:t + 1, :]

    for t in range(L):
        y = ydiag[t] + expcs_e[t] * yoff_scr[t] + dskip_ref[...] * xs[t]
        z = P[t][:, POOL_WIDTH:POOL_WIDTH + SSD_WIDTH]
        mix_ref[t, :, POOL_WIDTH:D_MODEL] = _gated_norm(y, z, ng_ref[...])


def _mix_sample(proj, dtp, sconv_t, spool_t, sssm, lw, l, nssm_stack):
    bt = SAMPLE_BT
    nb = DEC_BATCH // bt
    base = TP // bt
    aliased = nssm_stack is not None

    def pspec(t, width):
        return pl.BlockSpec((bt, width), lambda i, t=t: (base + t * nb + i, 0))

    full = lambda shape: pl.BlockSpec(shape, lambda i: (0,) * len(shape))
    return pl.pallas_call(
        functools.partial(_mixs_kernel, aliased=aliased),
        grid=(nb,),
        input_output_aliases={0: 2} if aliased else {},
        in_specs=([pl.BlockSpec(memory_space=pl.ANY)] if aliased else [])
        + [pspec(t, MAIN_WIDTH) for t in range(DEC_SEQ)] + [pspec(t, LANES) for t in range(DEC_SEQ)] + [
            pl.BlockSpec((None, SSD_CONV - 1, bt, XBC_WIDTH), lambda i: (l, 0, i, 0)),
            pl.BlockSpec((None, POOL_BUF, bt, POOL_WIDTH), lambda i: (l, 0, i, 0)),
            pl.BlockSpec((None, bt, SSD_HEADS, SSD_HEAD_DIM, D_STATE), lambda i: (l, i, 0, 0, 0)),
            full((len(POOL_WINDOWS), POOL_GROUP_DIM, POOL_GROUP_DIM)),
            full((1, POOL_WIDTH)),
            full((SSD_CONV, XBC_WIDTH)),
            full((1, XBC_WIDTH)),
            full((1, LANES)), full((1, LANES)),
            full((1, SSD_WIDTH)), full((1, SSD_WIDTH))],
        out_specs=[pl.BlockSpec((DEC_SEQ, bt, D_MODEL), lambda i: (0, i, 0)),
                   pl.BlockSpec((SSD_CONV - 1, bt, XBC_WIDTH), lambda i: (0, i, 0)),
                   pl.BlockSpec((None, bt, SSD_HEADS, SSD_HEAD_DIM, D_STATE), lambda i: (l, i, 0, 0, 0))],
        out_shape=[jax.ShapeDtypeStruct((DEC_SEQ, DEC_BATCH, D_MODEL), F32),
                   jax.ShapeDtypeStruct((SSD_CONV - 1, DEC_BATCH, XBC_WIDTH), F32),
                   jax.ShapeDtypeStruct((DEPTH, DEC_BATCH, SSD_HEADS, SSD_HEAD_DIM, D_STATE), F32)],
        scratch_shapes=[pltpu.VMEM((DEC_SEQ, bt, SSD_GROUPS * D_STATE), F32),
                        pltpu.VMEM((DEC_SEQ, bt, SSD_GROUPS * D_STATE), F32),
                        pltpu.VMEM((DEC_SEQ, bt, SSD_WIDTH), F32),
                        pltpu.VMEM((DEC_SEQ, bt, SSD_WIDTH), F32)],
        compiler_params=_cparams(("arbitrary",)),
        name="mix_sample",
    )(*([nssm_stack] if aliased else []),
      proj, proj, proj, proj, dtp, dtp, dtp, dtp, sconv_t, spool_t, sssm,
      lw["pool_w"], lw["pool_scale"], lw["conv_w"], lw["conv_b"], lw["dt_bias"], lw["a_log"],
      lw["d_skip_e"], lw["ssd_norm"])


RESID_PIECE_ROWS = 256


def _row_pieces(tm):
    return [slice(r0, r0 + RESID_PIECE_ROWS) for r0 in range(0, tm, RESID_PIECE_ROWS)]


def _resid_epilogue(acc, x_ref, r, gp, gs, ng_ref, mods, is_s):
    rows = r.stop - r.start
    xn = x_ref[r, :] + _rowmod(gp, gs, is_s, rows) * acc
    if mods is None:
        ms = jnp.mean(xn * xn, axis=-1, keepdims=True)
        return xn, xn * lax.rsqrt(ms + EPS) * ng_ref[...]
    scp, scs, shp, shs = mods
    return xn, _rms_mod(xn, ng_ref[...], _rowmod(scp, scs, is_s, rows), _rowmod(shp, shs, is_s, rows))


def _resid_kernel(ap_ref, as_ref, w_ref, x_ref, gp, gs, ng_ref, scp, scs, shp, shs, xo_ref, ho_ref, *, npt, tm):
    is_s = pl.program_id(0) >= npt
    for r in _row_pieces(tm):
        a = jnp.where(is_s, as_ref[r, :].astype(BF16), ap_ref[r, :])
        acc = jnp.dot(a, w_ref[...], preferred_element_type=F32)
        xn, hn = _resid_epilogue(acc, x_ref, r, gp, gs, ng_ref, (scp, scs, shp, shs), is_s)
        xo_ref[r, :] = xn
        ho_ref[r, :] = hn.astype(ho_ref.dtype)


def _final_kernel(ap_ref, as_ref, w_ref, x_ref, gp, gs, ng_ref, yp_ref, ys_ref, *, npt, tm):
    is_s = pl.program_id(0) >= npt
    a = jnp.where(is_s, as_ref[...].astype(BF16), ap_ref[...])
    acc = jnp.dot(a, w_ref[...], preferred_element_type=F32)
    _, yn = _resid_epilogue(acc, x_ref, slice(0, tm), gp, gs, ng_ref, None, is_s)

    @pl.when(jnp.logical_not(is_s))
    def _():
        yp_ref[...] = yn

    @pl.when(is_s)
    def _():
        ys_ref[...] = yn


def _resid_proj(a_p, a_s, w, x, gain, modp, mods, l, gate_k, next_mod):
    kdim = w.shape[-2]
    tm = 512 if kdim <= D_MODEL else 256
    npt = TP // tm
    row = lambda i: i
    if w.ndim == 3:
        w_spec = pl.BlockSpec((None, kdim, D_MODEL), lambda i: (l, 0, 0), pipeline_mode=pl.Buffered(1))
    else:
        w_spec = pl.BlockSpec((kdim, D_MODEL), lambda i: (0, 0), pipeline_mode=pl.Buffered(1))
    gp, gs = _mod_specs(l, gate_k, tm, row)
    kern = _resid_kernel
    in_specs = [pl.BlockSpec((tm, kdim), lambda i: (jnp.minimum(i, npt - 1), 0)),
                pl.BlockSpec((tm, kdim), lambda i: (jnp.maximum(i - npt, 0), 0)),
                w_spec,
                pl.BlockSpec((tm, D_MODEL), lambda i: (i, 0)),
                gp, gs,
                pl.BlockSpec((1, D_MODEL), lambda i: (0, 0))]
    args = [a_p, a_s, w, x, modp, mods, gain.reshape(1, D_MODEL)]
    if next_mod is None:
        kern = _final_kernel
        out_specs = [pl.BlockSpec((tm, D_MODEL), lambda i: (jnp.minimum(i, npt - 1), 0)),
                     pl.BlockSpec((tm, D_MODEL), lambda i: (jnp.maximum(i - npt, 0), 0))]
        out_shape = [jax.ShapeDtypeStruct((TP, D_MODEL), F32), jax.ShapeDtypeStruct((TS, D_MODEL), F32)]
    else:
        nl, shift_k, scale_k = next_mod
        scp, scs = _mod_specs(nl, scale_k, tm, row)
        shp, shs = _mod_specs(nl, shift_k, tm, row)
        in_specs += [scp, scs, shp, shs]
        args += [modp, mods, modp, mods]
        out_specs = [pl.BlockSpec((tm, D_MODEL), lambda i: (i, 0)),
                     pl.BlockSpec((tm, D_MODEL), lambda i: (i, 0))]
        out_shape = [jax.ShapeDtypeStruct((T, D_MODEL), F32), jax.ShapeDtypeStruct((T, D_MODEL), BF16)]
    return pl.pallas_call(
        functools.partial(kern, npt=npt, tm=tm),
        grid=(T // tm,),
        in_specs=in_specs,
        out_specs=out_specs,
        out_shape=out_shape,
        compiler_params=_cparams(("arbitrary",)),
        name="resid_proj",
    )(*args)


def _ffn_gate(gate, val, gm1, gm2, cw_ref, cb_ref, cs):
    gc = cb_ref[:, cs] + cw_ref[2:3, cs] * gate + cw_ref[1:2, cs] * gm1 + cw_ref[0:1, cs] * gm2
    return _silu(gc) * val


def _up_kernel(h_ref, hs_ref, wg_ref, wv0, wv1, wv2, wv3, cw_ref, cb_ref, sf_ref, wd_ref,
               act_ref, acts_ref, nfp_ref, nfs_ref, wdb_ref, wgb, wvb, gpad, *, tpb, tm, tn):
    m = pl.program_id(1)

    @pl.when(m == 0)
    def _():
        wgb[...] = wg_ref[...].astype(BF16)
        for k, r in enumerate((wv0, wv1, wv2, wv3)):
            wvb[:, k * LANES:(k + 1) * LANES] = r[...].astype(BF16)
        wdb_ref[...] = wd_ref[...].astype(BF16)

    first = (m % tpb) == 0
    hw = tn // 2
    plan = [(0, tm // 2, 0), (tm // 2, tm // 2, 0),
            (0, tm // 2, 1), (tm // 2, tm // 4, 1), (3 * tm // 4, tm // 4, 1)]
    h_halves = (h_ref[0:tm // 2, :], h_ref[tm // 2:tm, :])
    for r0, rows, half in plan:
        cs = slice(half * hw, (half + 1) * hw)
        hc = h_halves[r0 // (tm // 2)]
        if rows < tm // 2:
            q0 = r0 % (tm // 2)
            hc = hc[q0:q0 + rows]
        gate = jnp.dot(hc, wgb[:, cs], preferred_element_type=F32)
        val = jnp.dot(hc, wvb[:, cs], preferred_element_type=F32)
        if r0 == 0:
            gpad[0:8, cs] = jnp.where(first, 0.0, gpad[0:8, cs])
        gpad[8 + r0:8 + r0 + rows, cs] = gate
        gm1 = gpad[7 + r0:7 + r0 + rows, cs]
        gm2 = gpad[6 + r0:6 + r0 + rows, cs]
        act_ref[r0:r0 + rows, cs] = _ffn_gate(gate, val, gm1, gm2, cw_ref, cb_ref, cs).astype(act_ref.dtype)
        if r0 + rows == tm:
            gpad[0:8, cs] = gpad[tm:tm + 8, cs]
            nfp_ref[:, cs] = gate[rows - (FFN_CONV - 1):rows]

    @pl.when(m == pl.num_programs(1) - 1)
    def _():
        hs = hs_ref[...]
        B1, B2 = DEC_BATCH, 2 * DEC_BATCH
        for half in range(2):
            cs = slice(half * hw, (half + 1) * hw)
            gate = jnp.dot(hs, wgb[:, cs], preferred_element_type=F32)
            val = jnp.dot(hs, wvb[:, cs], preferred_element_type=F32)
            gm1 = jnp.concatenate([sf_ref[1, :, cs], gate[:TS - B1]], axis=0)
            gm2 = jnp.concatenate([sf_ref[0, :, cs], sf_ref[1, :, cs], gate[:TS - B2]], axis=0)
            acts_ref[:, cs] = _ffn_gate(gate, val, gm1, gm2, cw_ref, cb_ref, cs).astype(acts_ref.dtype)
            nfs_ref[0, :, cs] = gate[(DEC_SEQ - 2) * B1:(DEC_SEQ - 1) * B1]
            nfs_ref[1, :, cs] = gate[(DEC_SEQ - 1) * B1:DEC_SEQ * B1]


def _ffn_up(h2, w_up, conv_w, conv_b, sffn_t, w_down, l):
    tm, tn = 1024, 512
    assert SEQ % tm == 0 and DEC_SEQ == FFN_CONV + 1
    tpb = SEQ // tm
    nn = pl.cdiv(D_FF, tn)
    val0 = D_FF // LANES
    last = 2 * D_FF // LANES - 1

    def vspec(k):
        return pl.BlockSpec((None, D_MODEL, LANES),
                            lambda j, m: (l, 0, jnp.minimum(val0 + (tn // LANES) * j + k, last)))

    return pl.pallas_call(
        functools.partial(_up_kernel, tpb=tpb, tm=tm, tn=tn),
        grid=(nn, TP // tm),
        in_specs=[pl.BlockSpec((tm, D_MODEL), lambda j, m: (m, 0)),
                  pl.BlockSpec((TS, D_MODEL), lambda j, m: (TP // TS, 0)),
                  pl.BlockSpec((None, D_MODEL, tn), lambda j, m: (l, 0, j))]
                 + [vspec(k) for k in range(tn // LANES)]
                 + [pl.BlockSpec((FFN_CONV, tn), lambda j, m: (0, j)),
                    pl.BlockSpec((1, tn), lambda j, m: (0, j)),
                    pl.BlockSpec((None, FFN_CONV - 1, DEC_BATCH, tn), lambda j, m: (l, 0, 0, j)),
                    pl.BlockSpec((None, tn, D_MODEL), lambda j, m: (l, j, 0))],
        out_specs=[pl.BlockSpec((tm, tn), lambda j, m: (m, j)),
                   pl.BlockSpec((TS, tn), lambda j, m: (0, j)),
                   pl.BlockSpec((None, FFN_CONV - 1, tn), lambda j, m: (m // tpb, 0, j)),
                   pl.BlockSpec((FFN_CONV - 1, DEC_BATCH, tn), lambda j, m: (0, 0, j)),
                   pl.BlockSpec((tn, D_MODEL), lambda j, m: (j, 0))],
        out_shape=[jax.ShapeDtypeStruct((TP, D_FF), BF16),
                   jax.ShapeDtypeStruct((TS, D_FF), BF16),
                   jax.ShapeDtypeStruct((BATCH, FFN_CONV - 1, D_FF), F32),
                   jax.ShapeDtypeStruct((FFN_CONV - 1, DEC_BATCH, D_FF), F32),
                   jax.ShapeDtypeStruct((D_FF, D_MODEL), BF16)],
        scratch_shapes=[pltpu.VMEM((D_MODEL, tn), BF16), pltpu.VMEM((D_MODEL, tn), BF16),
                        pltpu.VMEM((8 + tm, tn), F32)],
        compiler_params=_cparams(("arbitrary", "arbitrary")),
        name="ffn_up",
    )(h2, h2, w_up, w_up, w_up, w_up, w_up, conv_w, conv_b.reshape(1, D_FF), sffn_t, w_down)


def _pad_lanes(v):
    return jnp.pad(v, ((0, 0), (0, LANES - v.shape[-1])))


def kernel(x_prompt, x_sample, c_prompt, c_sample, state_pool, state_conv, state_ssm, state_ffn, norm1, norm2, w_ada, b_ada, w_in, pool_w, pool_scale, conv_w, conv_b, dt_bias, a_log, d_skip, ssd_norm, w_out, w_up, ffn_conv_w, ffn_conv_b, w_down, norm_f):
    x = jnp.concatenate([x_prompt.reshape(TP, D_MODEL),
                         jnp.swapaxes(x_sample, 0, 1).reshape(TS, D_MODEL)], axis=0)
    c_all = jnp.concatenate([c_sample, c_prompt, jnp.zeros((4, D_MODEL), F32)], axis=0)
    mods, modp = _ada(c_all, w_ada, b_ada)
    modp = modp.reshape(DEPTH, BATCH, 1, 6 * D_MODEL)

    spool_t = jnp.swapaxes(state_pool, 1, 2)
    sconv_t = jnp.swapaxes(state_conv, 1, 2)
    sffn_t = jnp.swapaxes(state_ffn, 1, 2)

    w_in_t = jnp.swapaxes(w_in, 1, 2)

    d_skip_e = jnp.repeat(d_skip, SSD_HEAD_DIM, axis=-1)

    outs = {k: [] for k in ("pp", "pc", "ps", "pf", "sp", "sc", "sf")}
    h = _norm0(x, norm1[0], modp, mods, 0)
    nssm_stack = None
    for l in range(DEPTH):
        lw = {"pool_w": pool_w[l], "pool_scale": pool_scale[l].reshape(1, POOL_WIDTH),
              "conv_w": conv_w[l], "conv_b": conv_b[l].reshape(1, XBC_WIDTH),
              "dt_bias": _pad_lanes(dt_bias[l].reshape(1, SSD_HEADS)),
              "a_log": _pad_lanes(a_log[l].reshape(1, SSD_HEADS)),
              "d_skip_e": d_skip_e[l].reshape(1, SSD_WIDTH),
              "ssd_norm": ssd_norm[l].reshape(1, SSD_WIDTH)}
        proj, dtp = _inproj(h, w_in_t, l)
        mix_p, st_p, w_out_b = _mix_prompt(proj, dtp, lw, w_out, l)
        mix_s, nconv_s, nssm_stack = _mix_sample(proj, dtp, sconv_t, spool_t, state_ssm, lw, l, nssm_stack)
        x1, h2 = _resid_proj(mix_p, mix_s.reshape(TS, D_MODEL), w_out_b, x, norm2[l], modp, mods, l, 2, (l, 3, 4))
        act_p, act_s, nffn_p, nffn_s, w_down_b = _ffn_up(h2, w_up, ffn_conv_w[l], ffn_conv_b[l], sffn_t, w_down, l)
        if l + 1 < DEPTH:
            x, h = _resid_proj(act_p, act_s, w_down_b, x1, norm1[l + 1], modp, mods, l, 5, (l + 1, 0, 1))
        else:
            y_p, y_s = _resid_proj(act_p, act_s, w_down_b, x1, norm_f, modp, mods, l, 5, None)

        ends = [(b + 1) * SEQ for b in range(BATCH)]
        outs["pp"].append(jnp.stack([lax.slice(proj, (e - POOL_BUF, 0), (e, POOL_WIDTH)) for e in ends]))
        outs["pc"].append(jnp.stack([lax.slice(proj, (e - (SSD_CONV - 1), POOL_WIDTH + SSD_WIDTH), (e, MAIN_WIDTH))
                                     for e in ends]))
        outs["ps"].append(jnp.transpose(st_p.reshape(BATCH, D_STATE, SSD_HEADS, SSD_HEAD_DIM), (0, 2, 3, 1)))
        outs["pf"].append(nffn_p)
        u_s = jnp.swapaxes(lax.slice(proj, (TP, 0), (T, POOL_WIDTH)).reshape(DEC_SEQ, DEC_BATCH, POOL_WIDTH), 0, 1)
        outs["sp"].append(jnp.concatenate([state_pool[l][:, DEC_SEQ:], u_s], axis=1))
        outs["sc"].append(jnp.swapaxes(nconv_s, 0, 1))
        outs["sf"].append(jnp.swapaxes(nffn_s, 0, 1))

    y_prompt = y_p.reshape(BATCH, SEQ, D_MODEL)
    y_sample = jnp.swapaxes(y_s.reshape(DEC_SEQ, DEC_BATCH, D_MODEL), 0, 1)
    return (y_prompt, y_sample,
            jnp.stack(outs["pp"]), jnp.stack(outs["pc"]), jnp.stack(outs["ps"]), jnp.stack(outs["pf"]),
            jnp.stack(outs["sp"]), jnp.stack(outs["sc"]), nssm_stack, jnp.stack(outs["sf"]))
```

```python
import functools

import jax
import jax.numpy as jnp
from jax import lax
from jax.experimental import pallas as pl
from jax.experimental.pallas import tpu as pltpu

F32 = jnp.float32
BF16 = jnp.bfloat16

D_MODEL = 2048
BATCH = 4
SEQ = 2048
DEPTH = 4
DEC_BATCH = 128
DEC_SEQ = 4
PAST_LEN = 16384

POOL_WIDTH = 512
POOL_WINDOWS = (2, 4, 8, 16)
POOL_GROUP_DIM = 128
POOL_BUF = 15
SSD_WIDTH = 1536
SSD_HEAD_DIM = 64
SSD_HEADS = 24
SSD_GROUPS = 4
SSD_HPG = 6
D_STATE = 128
SSD_CONV = 4
XBC_WIDTH = 2560
MAIN_WIDTH = POOL_WIDTH + SSD_WIDTH + XBC_WIDTH
D_FF = 5504
FFN_CONV = 3
EPS = 1e-6

TP = BATCH * SEQ
TS = DEC_SEQ * DEC_BATCH
T = TP + TS
GROUP_COLS = SSD_HPG * SSD_HEAD_DIM
LANES = 128
BF16_SUBLANES = 16
CHUNK = 128

VMEM_LIMIT = 56 * 1024 * 1024


def _silu(x):
    return x * jax.nn.sigmoid(x)


def _softplus(x):
    return jnp.maximum(x, 0.0) + jnp.log1p(jnp.exp(-jnp.abs(x)))


def _cparams(sem):
    return pltpu.CompilerParams(dimension_semantics=sem, vmem_limit_bytes=VMEM_LIMIT)


def _rowmod(p_ref, s_ref, is_sample, tm):
    s = s_ref[...]
    if tm > DEC_BATCH:
        s = jnp.concatenate([s] * (tm // DEC_BATCH), axis=0)
    return jnp.where(is_sample, s, p_ref[...])


def _rms_mod(x, gain, sc, sh):
    ms = jnp.mean(x * x, axis=-1, keepdims=True)
    y = x * lax.rsqrt(ms + EPS) * gain
    return y * (1.0 + sc) + sh


def _ada_kernel(c_ref, w_ref, b_ref, os_ref, op_ref):
    a = _silu(c_ref[...]).astype(BF16)
    m = jnp.dot(a, w_ref[...].astype(BF16), preferred_element_type=F32) + b_ref[...]
    os_ref[...] = m[:DEC_BATCH]
    op_ref[...] = m[DEC_BATCH:DEC_BATCH + BATCH]


def _ada(c_all, w_ada, b_ada):
    tn = 1024
    rows = c_all.shape[0]
    return pl.pallas_call(
        _ada_kernel,
        grid=(DEPTH, 6 * D_MODEL // tn),
        in_specs=[
            pl.BlockSpec((rows, D_MODEL), lambda l, j: (0, 0)),
            pl.BlockSpec((None, D_MODEL, tn), lambda l, j: (l, 0, j)),
            pl.BlockSpec((None, 1, tn), lambda l, j: (l, 0, j)),
        ],
        out_specs=[
            pl.BlockSpec((None, DEC_BATCH, tn), lambda l, j: (l, 0, j)),
            pl.BlockSpec((None, BATCH, tn), lambda l, j: (l, 0, j)),
        ],
        out_shape=[
            jax.ShapeDtypeStruct((DEPTH, DEC_BATCH, 6 * D_MODEL), F32),
            jax.ShapeDtypeStruct((DEPTH, BATCH, 6 * D_MODEL), F32),
        ],
        compiler_params=_cparams(("arbitrary", "arbitrary")),
        name="ada",
    )(c_all, w_ada, b_ada.reshape(DEPTH, 1, 6 * D_MODEL))


def _mod_specs(l, k, tm, row_of):
    tpb = SEQ // tm

    def pmap(*ids):
        return (l, jnp.minimum(row_of(*ids) // tpb, BATCH - 1), 0, k)

    def smap(*ids):
        return (l, 0, k)

    return (pl.BlockSpec((None, None, 1, D_MODEL), pmap), pl.BlockSpec((None, DEC_BATCH, D_MODEL), smap))


def _norm_kernel(x_ref, g_ref, scp, scs, shp, shs, o_ref, *, npt, tm):
    is_s = pl.program_id(0) >= npt
    sc = _rowmod(scp, scs, is_s, tm)
    sh = _rowmod(shp, shs, is_s, tm)
    o_ref[...] = _rms_mod(x_ref[...], g_ref[...], sc, sh).astype(o_ref.dtype)


def _norm0(x, gain, modp, mods, l):
    tm = 256
    npt = TP // tm
    row = lambda i: i
    scp, scs = _mod_specs(l, 1, tm, row)
    shp, shs = _mod_specs(l, 0, tm, row)
    return pl.pallas_call(
        functools.partial(_norm_kernel, npt=npt, tm=tm),
        grid=(T // tm,),
        in_specs=[pl.BlockSpec((tm, D_MODEL), lambda i: (i, 0)),
                  pl.BlockSpec((1, D_MODEL), lambda i: (0, 0)),
                  scp, scs, shp, shs],
        out_specs=pl.BlockSpec((tm, D_MODEL), lambda i: (i, 0)),
        out_shape=jax.ShapeDtypeStruct((T, D_MODEL), BF16),
        compiler_params=_cparams(("arbitrary",)),
        name="norm0",
    )(x, gain.reshape(1, D_MODEL), modp, mods, modp, mods)


_NT = (((1,), (1,)), ((), ()))


def _inproj_kernel(h_ref, w_ref, wdt_ref, o_ref, dt_ref, wb, wdtb):
    j = pl.program_id(0)
    m = pl.program_id(1)

    @pl.when(m == 0)
    def _():
        wb[...] = w_ref[...].astype(BF16)

    h = h_ref[...]
    o_ref[...] = lax.dot_general(h, wb[...], _NT, preferred_element_type=F32)

    @pl.when(j == 0)
    def _():
        @pl.when(m == 0)
        def _():
            row = lax.broadcasted_iota(jnp.int32, (LANES, 1), 0)
            wdtb[...] = jnp.where(row < SSD_HEADS, wdt_ref[...], 0.0).astype(BF16)

        dt_ref[...] = lax.dot_general(h, wdtb[...], _NT, preferred_element_type=F32)

    @pl.when(j != 0)
    def _():
        dt_ref[...] = jnp.zeros(dt_ref.shape, F32)


def _inproj(h, w_in_t, l):
    tm, tn = 512, 1536
    nm = T // tm
    return pl.pallas_call(
        _inproj_kernel,
        grid=(MAIN_WIDTH // tn, nm),
        in_specs=[pl.BlockSpec((tm, D_MODEL), lambda j, m: (m, 0)),
                  pl.BlockSpec((None, tn, D_MODEL), lambda j, m: (l, j, 0)),
                  pl.BlockSpec((None, LANES, D_MODEL), lambda j, m: (l, MAIN_WIDTH // LANES, 0))],
        out_specs=[pl.BlockSpec((tm, tn), lambda j, m: (m, j)),
                   pl.BlockSpec((tm, LANES), lambda j, m: (jnp.where(j == 0, m, nm), 0))],
        out_shape=[jax.ShapeDtypeStruct((T, MAIN_WIDTH), F32),
                   jax.ShapeDtypeStruct((T + tm, LANES), F32)],
        scratch_shapes=[pltpu.VMEM((tn, D_MODEL), BF16), pltpu.VMEM((LANES, D_MODEL), BF16)],
        compiler_params=_cparams(("arbitrary", "arbitrary")),
        name="inproj",
    )(h, w_in_t, w_in_t)


def _expand_heads(v, rows):
    lo = lax.broadcasted_iota(jnp.int32, (rows, LANES), 1) < SSD_HEAD_DIM
    parts = []
    for j in range(SSD_HEADS // 2):
        a = jnp.broadcast_to(v[:, 2 * j:2 * j + 1], (rows, LANES))
        b = jnp.broadcast_to(v[:, 2 * j + 1:2 * j + 2], (rows, LANES))
        parts.append(jnp.where(lo, a, b))
    return jnp.concatenate(parts, axis=1)


def _split3(x):
    a = x.astype(BF16)
    r = x - a.astype(F32)
    b = r.astype(BF16)
    c = (r - b.astype(F32)).astype(BF16)
    return a, b, c


def _gated_norm(y, z, gain):
    y = y * _silu(z)
    ms = jnp.mean(y * y, axis=-1, keepdims=True)
    return y * lax.rsqrt(ms + EPS) * gain


PROMPT_CHUNKS_PER_STEP = 2


def _mixp_kernel(proj_ref, dt_ref, poolw_ref, pscale_ref, cw_ref, cb_ref, dtb_ref, alog_ref, dskip_ref, ng_ref, hc_ref,
                 wo_ref, mix_ref, st_ref, wob_ref, xpad, upad, hT):
    c = pl.program_id(1)

    @pl.when(c == 0)
    def _():
        xpad[0:8, :] = jnp.zeros((8, XBC_WIDTH), F32)
        upad[0:16, :] = jnp.zeros((16, POOL_WIDTH), F32)
        hT[...] = jnp.zeros(hT.shape, F32)

    wob_ref[...] = wo_ref[...].astype(BF16)

    for sub in range(PROMPT_CHUNKS_PER_STEP):
        _mixp_chunk(c * PROMPT_CHUNKS_PER_STEP + sub, slice(sub * CHUNK, (sub + 1) * CHUNK),
                    proj_ref, dt_ref, poolw_ref, pscale_ref, cw_ref, cb_ref, dtb_ref, alog_ref, dskip_ref, ng_ref,
                    hc_ref, mix_ref, xpad, upad, hT)

    @pl.when(c == pl.num_programs(1) - 1)
    def _():
        st_ref[...] = hT[...]


def _mixp_chunk(chunk, r, proj_ref, dt_ref, poolw_ref, pscale_ref, cw_ref, cb_ref, dtb_ref, alog_ref, dskip_ref, ng_ref,
                hc_ref, mix_ref, xpad, upad, hT):
    Q = CHUNK

    u = proj_ref[r, 0:POOL_WIDTH]
    upad[16:16 + Q, :] = u
    pos = chunk * Q + lax.broadcasted_iota(jnp.int32, (Q, 1), 0)
    outs = []
    for g, w in enumerate(POOL_WINDOWS):
        lo = g * POOL_GROUP_DIM
        ug = u[:, lo:lo + POOL_GROUP_DIM]
        s = ug
        for k in range(1, w):
            s = s + upad[16 - k:16 - k + Q, lo:lo + POOL_GROUP_DIM]
        cnt = jnp.minimum(pos + 1, w).astype(F32)
        d = s / cnt - ug
        outs.append(jnp.dot(d.astype(BF16), poolw_ref[g].astype(BF16), preferred_element_type=F32))
    mix_ref[r, 0:POOL_WIDTH] = (jnp.concatenate(outs, axis=1) * pscale_ref[...]).astype(mix_ref.dtype)
    upad[0:16, :] = upad[Q:Q + 16, :]

    xbc = proj_ref[r, POOL_WIDTH + SSD_WIDTH:MAIN_WIDTH]
    xpad[8:8 + Q, :] = xbc
    acc = cb_ref[...] + cw_ref[3:4, :] * xbc
    for i in range(SSD_CONV - 1):
        acc = acc + cw_ref[i:i + 1, :] * xpad[5 + i:5 + i + Q, :]
    xpad[0:8, :] = xpad[Q:Q + 8, :]
    xc = _silu(acc)
    xs = xc[:, 0:SSD_WIDTH]
    Bb = xc[:, SSD_WIDTH:SSD_WIDTH + SSD_GROUPS * D_STATE].astype(BF16)
    Cb = xc[:, SSD_WIDTH + SSD_GROUPS * D_STATE:XBC_WIDTH].astype(BF16)

    dt = _softplus(dt_ref[r, :] + dtb_ref[...])
    dA = dt * (-jnp.exp(alog_ref[...]))
    row = lax.broadcasted_iota(jnp.int32, (Q, Q), 0)
    col = lax.broadcasted_iota(jnp.int32, (Q, Q), 1)
    tri = row >= col
    tri_lo = jnp.where(tri, 1.0, 0.0).astype(BF16)
    tri_up = jnp.where(row <= col, 1.0, 0.0).astype(BF16)
    pieces = _split3(dA)
    cs = sum(jnp.dot(tri_lo, p, preferred_element_type=F32) for p in pieces)
    csT = sum(lax.dot_general(p, tri_up, (((0,), (0,)), ((), ())), preferred_element_type=F32)
              for p in pieces)
    cs_last = cs[Q - 1:Q, :]
    stacked = jnp.concatenate([dt, jnp.exp(cs), jnp.exp(cs_last - cs)], axis=0)
    head_cols = hc_ref[...]
    expanded = sum(jnp.dot(p, head_cols, preferred_element_type=F32) for p in _split3(stacked))
    expcs_e = expanded[Q:2 * Q]
    dte_e = expanded[2 * Q:3 * Q]
    xdt = xs * expanded[0:Q]
    xdt_b = xdt.astype(BF16)
    lo_half = lax.broadcasted_iota(jnp.int32, (Q, LANES), 1) < SSD_HEAD_DIM
    xw_b = (xdt * dte_e).astype(BF16)
    decay_e = expcs_e[Q - 1:Q, :]
    h_prev = hT[...]
    h_prev_b = h_prev.astype(BF16)

    ys = []
    for g in range(SSD_GROUPS):
        gs = slice(g * GROUP_COLS, (g + 1) * GROUP_COLS)
        Cg = Cb[:, g * D_STATE:(g + 1) * D_STATE]
        Bg = Bb[:, g * D_STATE:(g + 1) * D_STATE]
        CB = lax.dot_general(Cg, Bg, (((1,), (1,)), ((), ())), preferred_element_type=F32)
        y_off = jnp.dot(Cg, h_prev_b[:, gs], preferred_element_type=F32)
        ST = lax.dot_general(Bg, xw_b[:, gs], (((0,), (0,)), ((), ())), preferred_element_type=F32)
        hT[:, gs] = h_prev[:, gs] * decay_e[:, gs] + ST
        parts = []
        for j in range(SSD_HPG // 2):
            h1 = SSD_HPG * g + 2 * j
            ps = slice(h1 * SSD_HEAD_DIM, h1 * SSD_HEAD_DIM + LANES)
            W = []
            for h in (h1, h1 + 1):
                seg = cs[:, h:h + 1] - csT[h:h + 1, :]
                W.append((CB * jnp.exp(jnp.where(tri, seg, -jnp.inf))).astype(BF16))
            parts.append(jnp.where(lo_half,
                                   jnp.dot(W[0], xdt_b[:, ps], preferred_element_type=F32),
                                   jnp.dot(W[1], xdt_b[:, ps], preferred_element_type=F32)))
        ys.append(jnp.concatenate(parts, axis=1) + y_off * expcs_e[:, gs])
    y = jnp.concatenate(ys, axis=1) + dskip_ref[...] * xs
    z = proj_ref[r, POOL_WIDTH:POOL_WIDTH + SSD_WIDTH]
    mix_ref[r, POOL_WIDTH:D_MODEL] = _gated_norm(y, z, ng_ref[...]).astype(mix_ref.dtype)


def _mix_prompt(proj, dtp, lw, w_out, l):
    rows = CHUNK * PROMPT_CHUNKS_PER_STEP
    nc = SEQ // rows
    wrows = D_MODEL // (BATCH * nc)
    full = lambda shape: pl.BlockSpec(shape, lambda b, c: (0,) * len(shape))
    return pl.pallas_call(
        _mixp_kernel,
        grid=(BATCH, nc),
        in_specs=[pl.BlockSpec((rows, MAIN_WIDTH), lambda b, c: (b * nc + c, 0)),
                  pl.BlockSpec((rows, LANES), lambda b, c: (b * nc + c, 0)),
                  full((len(POOL_WINDOWS), POOL_GROUP_DIM, POOL_GROUP_DIM)),
                  full((1, POOL_WIDTH)),
                  full((SSD_CONV, XBC_WIDTH)),
                  full((1, XBC_WIDTH)),
                  full((1, LANES)), full((1, LANES)),
                  full((1, SSD_WIDTH)), full((1, SSD_WIDTH)),
                  full((LANES, SSD_WIDTH)),
                  pl.BlockSpec((None, wrows, D_MODEL), lambda b, c: (l, b * nc + c, 0))],
        out_specs=[pl.BlockSpec((rows, D_MODEL), lambda b, c: (b * nc + c, 0)),
                   pl.BlockSpec((None, D_STATE, SSD_WIDTH), lambda b, c: (b, 0, 0)),
                   pl.BlockSpec((wrows, D_MODEL), lambda b, c: (b * nc + c, 0))],
        out_shape=[jax.ShapeDtypeStruct((TP, D_MODEL), BF16),
                   jax.ShapeDtypeStruct((BATCH, D_STATE, SSD_WIDTH), F32),
                   jax.ShapeDtypeStruct((D_MODEL, D_MODEL), BF16)],
        scratch_shapes=[pltpu.VMEM((8 + CHUNK, XBC_WIDTH), F32),
                        pltpu.VMEM((16 + CHUNK, POOL_WIDTH), F32),
                        pltpu.VMEM((D_STATE, SSD_WIDTH), F32)],
        compiler_params=_cparams(("arbitrary", "arbitrary")),
        name="mix_prompt",
    )(proj, dtp, lw["pool_w"], lw["pool_scale"], lw["conv_w"], lw["conv_b"], lw["dt_bias"], lw["a_log"],
      lw["d_skip_e"], lw["ssd_norm"], _head_cols(), w_out)


def _head_cols():
    head_of_col = jnp.arange(SSD_WIDTH, dtype=jnp.int32) // SSD_HEAD_DIM
    return (head_of_col[None, :] == jnp.arange(LANES, dtype=jnp.int32)[:, None]).astype(BF16)


SAMPLE_BT = 8


def _mixs_kernel(*refs, aliased):
    if aliased:
        refs = refs[1:]
    (p0, p1, p2, p3, d0, d1, d2, d3, sconv_ref, spool_ref, sssm_ref,
     poolw_ref, pscale_ref, cw_ref, cb_ref, dtb_ref, alog_ref, dskip_ref, ng_ref,
     mix_ref, nconv_ref, nssm_ref, c_scr, b_scr, xw_scr, yoff_scr) = refs
    bt = SAMPLE_BT
    P = (p0, p1, p2, p3)
    Dt = (d0, d1, d2, d3)
    L = DEC_SEQ

    fullp = [spool_ref[k] for k in range(POOL_BUF)] + [P[t][:, 0:POOL_WIDTH] for t in range(L)]
    for g, w in enumerate(POOL_WINDOWS):
        gl = slice(g * POOL_GROUP_DIM, (g + 1) * POOL_GROUP_DIM)
        ds = []
        for t in range(L):
            s = fullp[POOL_BUF + t][:, gl]
            for k in range(1, w):
                s = s + fullp[POOL_BUF + t - k][:, gl]
            cnt = float(min(PAST_LEN + t + 1, w))
            ds.append(s / cnt - fullp[POOL_BUF + t][:, gl])
        o = jnp.dot(jnp.concatenate(ds, axis=0).astype(BF16), poolw_ref[g].astype(BF16),
                    preferred_element_type=F32) * pscale_ref[:, gl]
        for t in range(L):
            mix_ref[t, :, gl] = o[t * bt:(t + 1) * bt]

    fullc = [sconv_ref[k] for k in range(SSD_CONV - 1)] + [P[t][:, POOL_WIDTH + SSD_WIDTH:MAIN_WIDTH] for t in range(L)]
    for k in range(SSD_CONV - 1):
        nconv_ref[k] = fullc[L + k]
    xs, Bm, Cm = [], [], []
    for t in range(L):
        acc = cb_ref[...]
        for i in range(SSD_CONV):
            acc = acc + cw_ref[i:i + 1, :] * fullc[t + i]
        xc = _silu(acc)
        xs.append(xc[:, 0:SSD_WIDTH])
        Bm.append(xc[:, SSD_WIDTH:SSD_WIDTH + SSD_GROUPS * D_STATE])
        Cm.append(xc[:, SSD_WIDTH + SSD_GROUPS * D_STATE:XBC_WIDTH])

    A = -jnp.exp(alog_ref[...])
    dts = [_softplus(Dt[t][...] + dtb_ref[...]) for t in range(L)]
    cs = []
    for t in range(L):
        dA = dts[t] * A
        cs.append(dA if t == 0 else cs[t - 1] + dA)
    dec24 = jnp.exp(cs[L - 1])
    expcs_e = [_expand_heads(jnp.exp(cs[t]), bt) for t in range(L)]
    xdt = [xs[t] * _expand_heads(dts[t], bt) for t in range(L)]

    ydiag = []
    for t in range(L):
        acc = None
        for s in range(t + 1):
            cb_parts = []
            for g in range(SSD_GROUPS):
                sl = slice(g * D_STATE, (g + 1) * D_STATE)
                r = jnp.sum(Cm[t][:, sl] * Bm[s][:, sl], axis=-1, keepdims=True)
                cb_parts.append(jnp.broadcast_to(r, (bt, GROUP_COLS)))
            coef = jnp.concatenate(cb_parts, axis=1)
            if s < t:
                coef = coef * _expand_heads(jnp.exp(cs[t] - cs[s]), bt)
            term = coef * xdt[s]
            acc = term if acc is None else acc + term
        ydiag.append(acc)

    for t in range(L):
        c_scr[t] = Cm[t]
        b_scr[t] = Bm[t]
        xw_scr[t] = xdt[t] * _expand_heads(jnp.exp(cs[L - 1] - cs[t]), bt)
    zpad_s = jnp.zeros((8 - L, D_STATE), F32)
    zpad_x = jnp.zeros((8 - L, GROUP_COLS), F32)
    for b in range(bt):
        for g in range(SSD_GROUPS):
            sl = slice(g * D_STATE, (g + 1) * D_STATE)
            gs = slice(g * GROUP_COLS, (g + 1) * GROUP_COLS)
            C8 = jnp.concatenate([c_scr[t, b:b + 1, sl] for t in range(L)] + [zpad_s], axis=0)
            B8 = jnp.concatenate([b_scr[t, b:b + 1, sl] for t in range(L)] + [zpad_s], axis=0)
            X8 = jnp.concatenate([xw_scr[t, b:b + 1, gs] for t in range(L)] + [zpad_x], axis=0)
            h0 = sssm_ref[b, SSD_HPG * g:SSD_HPG * (g + 1)].reshape(GROUP_COLS, D_STATE)
            GT = lax.dot_general(C8.astype(BF16), h0.astype(BF16), (((1,), (1,)), ((), ())),
                                 preferred_element_type=F32)
            U = lax.dot_general(X8, B8, (((0,), (0,)), ((), ())), preferred_element_type=F32)
            for r in range(SSD_HPG):
                h = SSD_HPG * g + r
                dec = jnp.broadcast_to(dec24[b:b + 1, h:h + 1], (SSD_HEAD_DIM, D_STATE))
                rs = slice(r * SSD_HEAD_DIM, (r + 1) * SSD_HEAD_DIM)
                nssm_ref[b, h] = h0[rs] * dec + U[rs]
            for t in range(L):
                yoff_scr[t, b:b + 1, gs] = GT[t:t + 1, :]

    for t in range(L):
        y = ydiag[t] + expcs_e[t] * yoff_scr[t] + dskip_ref[...] * xs[t]
        z = P[t][:, POOL_WIDTH:POOL_WIDTH + SSD_WIDTH]
        mix_ref[t, :, POOL_WIDTH:D_MODEL] = _gated_norm(y, z, ng_ref[...])


def _mix_sample(proj, dtp, sconv_t, spool_t, sssm, lw, l, nssm_stack):
    bt = SAMPLE_BT
    nb = DEC_BATCH // bt
    base = TP // bt
    aliased = nssm_stack is not None

    def pspec(t, width):
        return pl.BlockSpec((bt, width), lambda i, t=t: (base + t * nb + i, 0))

    full = lambda shape: pl.BlockSpec(shape, lambda i: (0,) * len(shape))
    return pl.pallas_call(
        functools.partial(_mixs_kernel, aliased=aliased),
        grid=(nb,),
        input_output_aliases={0: 2} if aliased else {},
        in_specs=([pl.BlockSpec(memory_space=pl.ANY)] if aliased else [])
        + [pspec(t, MAIN_WIDTH) for t in range(DEC_SEQ)] + [pspec(t, LANES) for t in range(DEC_SEQ)] + [
            pl.BlockSpec((None, SSD_CONV - 1, bt, XBC_WIDTH), lambda i: (l, 0, i, 0)),
            pl.BlockSpec((None, POOL_BUF, bt, POOL_WIDTH), lambda i: (l, 0, i, 0)),
            pl.BlockSpec((None, bt, SSD_HEADS, SSD_HEAD_DIM, D_STATE), lambda i: (l, i, 0, 0, 0)),
            full((len(POOL_WINDOWS), POOL_GROUP_DIM, POOL_GROUP_DIM)),
            full((1, POOL_WIDTH)),
            full((SSD_CONV, XBC_WIDTH)),
            full((1, XBC_WIDTH)),
            full((1, LANES)), full((1, LANES)),
            full((1, SSD_WIDTH)), full((1, SSD_WIDTH))],
        out_specs=[pl.BlockSpec((DEC_SEQ, bt, D_MODEL), lambda i: (0, i, 0)),
                   pl.BlockSpec((SSD_CONV - 1, bt, XBC_WIDTH), lambda i: (0, i, 0)),
                   pl.BlockSpec((None, bt, SSD_HEADS, SSD_HEAD_DIM, D_STATE), lambda i: (l, i, 0, 0, 0))],
        out_shape=[jax.ShapeDtypeStruct((DEC_SEQ, DEC_BATCH, D_MODEL), F32),
                   jax.ShapeDtypeStruct((SSD_CONV - 1, DEC_BATCH, XBC_WIDTH), F32),
                   jax.ShapeDtypeStruct((DEPTH, DEC_BATCH, SSD_HEADS, SSD_HEAD_DIM, D_STATE), F32)],
        scratch_shapes=[pltpu.VMEM((DEC_SEQ, bt, SSD_GROUPS * D_STATE), F32),
                        pltpu.VMEM((DEC_SEQ, bt, SSD_GROUPS * D_STATE), F32),
                        pltpu.VMEM((DEC_SEQ, bt, SSD_WIDTH), F32),
                        pltpu.VMEM((DEC_SEQ, bt, SSD_WIDTH), F32)],
        compiler_params=_cparams(("arbitrary",)),
        name="mix_sample",
    )(*([nssm_stack] if aliased else []),
      proj, proj, proj, proj, dtp, dtp, dtp, dtp, sconv_t, spool_t, sssm,
      lw["pool_w"], lw["pool_scale"], lw["conv_w"], lw["conv_b"], lw["dt_bias"], lw["a_log"],
      lw["d_skip_e"], lw["ssd_norm"])


RESID_PIECE_ROWS = 256


def _row_pieces(tm):
    return [slice(r0, r0 + RESID_PIECE_ROWS) for r0 in range(0, tm, RESID_PIECE_ROWS)]


def _resid_epilogue(acc, x_ref, r, gp, gs, ng_ref, mods, is_s):
    rows = r.stop - r.start
    xn = x_ref[r, :] + _rowmod(gp, gs, is_s, rows) * acc
    if mods is None:
        ms = jnp.mean(xn * xn, axis=-1, keepdims=True)
        return xn, xn * lax.rsqrt(ms + EPS) * ng_ref[...]
    scp, scs, shp, shs = mods
    return xn, _rms_mod(xn, ng_ref[...], _rowmod(scp, scs, is_s, rows), _rowmod(shp, shs, is_s, rows))


def _resid_kernel(ap_ref, as_ref, w_ref, x_ref, gp, gs, ng_ref, scp, scs, shp, shs, *rest, npt, tm):
    if len(rest) == 4:
        side_in, xo_ref, ho_ref, side_out = rest
        side_out[...] = side_in[...].astype(BF16)
    else:
        xo_ref, ho_ref = rest
    is_s = pl.program_id(0) >= npt
    for r in _row_pieces(tm):
        a = jnp.where(is_s, as_ref[r, :].astype(BF16), ap_ref[r, :])
        acc = jnp.dot(a, w_ref[...], preferred_element_type=F32)
        xn, hn = _resid_epilogue(acc, x_ref, r, gp, gs, ng_ref, (scp, scs, shp, shs), is_s)
        xo_ref[r, :] = xn
        ho_ref[r, :] = hn.astype(ho_ref.dtype)


def _final_kernel(ap_ref, as_ref, w_ref, x_ref, gp, gs, ng_ref, yp_ref, ys_ref, *, npt, tm):
    is_s = pl.program_id(0) >= npt
    a = jnp.where(is_s, as_ref[...].astype(BF16), ap_ref[...])
    acc = jnp.dot(a, w_ref[...], preferred_element_type=F32)
    _, yn = _resid_epilogue(acc, x_ref, slice(0, tm), gp, gs, ng_ref, None, is_s)

    @pl.when(jnp.logical_not(is_s))
    def _():
        yp_ref[...] = yn

    @pl.when(is_s)
    def _():
        ys_ref[...] = yn


def _resid_proj(a_p, a_s, w, x, gain, modp, mods, l, gate_k, next_mod, side_w=None):
    kdim = w.shape[-2]
    tm = 512 if kdim <= D_MODEL else 256
    npt = TP // tm
    row = lambda i: i
    if w.ndim == 3:
        w_spec = pl.BlockSpec((None, kdim, D_MODEL), lambda i: (l, 0, 0), pipeline_mode=pl.Buffered(1))
    else:
        w_spec = pl.BlockSpec((kdim, D_MODEL), lambda i: (0, 0), pipeline_mode=pl.Buffered(1))
    gp, gs = _mod_specs(l, gate_k, tm, row)
    kern = _resid_kernel
    in_specs = [pl.BlockSpec((tm, kdim), lambda i: (jnp.minimum(i, npt - 1), 0)),
                pl.BlockSpec((tm, kdim), lambda i: (jnp.maximum(i - npt, 0), 0)),
                w_spec,
                pl.BlockSpec((tm, D_MODEL), lambda i: (i, 0)),
                gp, gs,
                pl.BlockSpec((1, D_MODEL), lambda i: (0, 0))]
    args = [a_p, a_s, w, x, modp, mods, gain.reshape(1, D_MODEL)]
    if next_mod is None:
        kern = _final_kernel
        out_specs = [pl.BlockSpec((tm, D_MODEL), lambda i: (jnp.minimum(i, npt - 1), 0)),
                     pl.BlockSpec((tm, D_MODEL), lambda i: (jnp.maximum(i - npt, 0), 0))]
        out_shape = [jax.ShapeDtypeStruct((TP, D_MODEL), F32), jax.ShapeDtypeStruct((TS, D_MODEL), F32)]
    else:
        nl, shift_k, scale_k = next_mod
        scp, scs = _mod_specs(nl, scale_k, tm, row)
        shp, shs = _mod_specs(nl, shift_k, tm, row)
        in_specs += [scp, scs, shp, shs]
        args += [modp, mods, modp, mods]
        out_specs = [pl.BlockSpec((tm, D_MODEL), lambda i: (i, 0)),
                     pl.BlockSpec((tm, D_MODEL), lambda i: (i, 0))]
        out_shape = [jax.ShapeDtypeStruct((T, D_MODEL), F32), jax.ShapeDtypeStruct((T, D_MODEL), BF16)]
        if side_w is not None:
            srows = side_w.shape[1]
            blk = BF16_SUBLANES * pl.cdiv(srows, BF16_SUBLANES * (T // tm))
            in_specs.append(pl.BlockSpec((None, blk, D_MODEL), lambda i: (l, i, 0)))
            args.append(side_w)
            out_specs.append(pl.BlockSpec((blk, D_MODEL), lambda i: (i, 0)))
            out_shape.append(jax.ShapeDtypeStruct((srows, D_MODEL), BF16))
    return pl.pallas_call(
        functools.partial(kern, npt=npt, tm=tm),
        grid=(T // tm,),
        in_specs=in_specs,
        out_specs=out_specs,
        out_shape=out_shape,
        compiler_params=_cparams(("arbitrary",)),
        name="resid_proj",
    )(*args)


def _ffn_gate(gate, val, gm1, gm2, cw_ref, cb_ref, cs):
    gc = cb_ref[:, cs] + cw_ref[2:3, cs] * gate + cw_ref[1:2, cs] * gm1 + cw_ref[0:1, cs] * gm2
    return _silu(gc) * val


UP_PIECE_ROWS = 512


def _up_kernel(h_ref, hs_ref, wg_ref, wv0, wv1, wv2, wv3, cw_ref, cb_ref, sf_ref,
               act_ref, acts_ref, nfp_ref, nfs_ref, wgb, wvb, gpad, *, tpb, tm, tn):
    m = pl.program_id(1)

    @pl.when(m == 0)
    def _():
        wgb[...] = wg_ref[...].astype(BF16)
        for k, r in enumerate((wv0, wv1, wv2, wv3)):
            wvb[:, k * LANES:(k + 1) * LANES] = r[...].astype(BF16)

    first = (m % tpb) == 0
    hw = tn // 2
    P = UP_PIECE_ROWS
    plan = [(r0, P, 0) for r0 in range(0, tm, P)] + [(r0, P, 1) for r0 in range(0, tm - P, P)]
    plan += [(tm - P, P // 2, 1), (tm - P // 2, P // 2, 1)]
    for r0, rows, half in plan:
        cs = slice(half * hw, (half + 1) * hw)
        hc = h_ref[r0:r0 + rows, :]
        gate = jnp.dot(hc, wgb[:, cs], preferred_element_type=F32)
        val = jnp.dot(hc, wvb[:, cs], preferred_element_type=F32)
        if r0 == 0:
            gpad[0:8, cs] = jnp.where(first, 0.0, gpad[0:8, cs])
        gpad[8 + r0:8 + r0 + rows, cs] = gate
        gm1 = gpad[7 + r0:7 + r0 + rows, cs]
        gm2 = gpad[6 + r0:6 + r0 + rows, cs]
        act_ref[r0:r0 + rows, cs] = _ffn_gate(gate, val, gm1, gm2, cw_ref, cb_ref, cs).astype(act_ref.dtype)
        if r0 + rows == tm:
            gpad[0:8, cs] = gpad[tm:tm + 8, cs]
            nfp_ref[:, cs] = gate[rows - (FFN_CONV - 1):rows]

    @pl.when(m == pl.num_programs(1) - 1)
    def _():
        hs = hs_ref[...]
        B1, B2 = DEC_BATCH, 2 * DEC_BATCH
        for half in range(2):
            cs = slice(half * hw, (half + 1) * hw)
            gate = jnp.dot(hs, wgb[:, cs], preferred_element_type=F32)
            val = jnp.dot(hs, wvb[:, cs], preferred_element_type=F32)
            gm1 = jnp.concatenate([sf_ref[1, :, cs], gate[:TS - B1]], axis=0)
            gm2 = jnp.concatenate([sf_ref[0, :, cs], sf_ref[1, :, cs], gate[:TS - B2]], axis=0)
            acts_ref[:, cs] = _ffn_gate(gate, val, gm1, gm2, cw_ref, cb_ref, cs).astype(acts_ref.dtype)
            nfs_ref[0, :, cs] = gate[(DEC_SEQ - 2) * B1:(DEC_SEQ - 1) * B1]
            nfs_ref[1, :, cs] = gate[(DEC_SEQ - 1) * B1:DEC_SEQ * B1]


def _ffn_up(h2, w_up, conv_w, conv_b, sffn_t, l):
    tm, tn = SEQ, 512
    assert tm % UP_PIECE_ROWS == 0 and DEC_SEQ == FFN_CONV + 1
    tpb = SEQ // tm
    nn = pl.cdiv(D_FF, tn)
    val0 = D_FF // LANES
    last = 2 * D_FF // LANES - 1

    def vspec(k):
        return pl.BlockSpec((None, D_MODEL, LANES),
                            lambda j, m: (l, 0, jnp.minimum(val0 + (tn // LANES) * j + k, last)))

    return pl.pallas_call(
        functools.partial(_up_kernel, tpb=tpb, tm=tm, tn=tn),
        grid=(nn, TP // tm),
        in_specs=[pl.BlockSpec((tm, D_MODEL), lambda j, m: (m, 0)),
                  pl.BlockSpec((TS, D_MODEL), lambda j, m: (TP // TS, 0), pipeline_mode=pl.Buffered(1)),
                  pl.BlockSpec((None, D_MODEL, tn), lambda j, m: (l, 0, j))]
                 + [vspec(k) for k in range(tn // LANES)]
                 + [pl.BlockSpec((FFN_CONV, tn), lambda j, m: (0, j)),
                    pl.BlockSpec((1, tn), lambda j, m: (0, j)),
                    pl.BlockSpec((None, FFN_CONV - 1, DEC_BATCH, tn), lambda j, m: (l, 0, 0, j))],
        out_specs=[pl.BlockSpec((tm, tn), lambda j, m: (m, j)),
                   pl.BlockSpec((TS, tn), lambda j, m: (0, j)),
                   pl.BlockSpec((None, FFN_CONV - 1, tn), lambda j, m: (m // tpb, 0, j)),
                   pl.BlockSpec((FFN_CONV - 1, DEC_BATCH, tn), lambda j, m: (0, 0, j))],
        out_shape=[jax.ShapeDtypeStruct((TP, D_FF), BF16),
                   jax.ShapeDtypeStruct((TS, D_FF), BF16),
                   jax.ShapeDtypeStruct((BATCH, FFN_CONV - 1, D_FF), F32),
                   jax.ShapeDtypeStruct((FFN_CONV - 1, DEC_BATCH, D_FF), F32)],
        scratch_shapes=[pltpu.VMEM((D_MODEL, tn), BF16), pltpu.VMEM((D_MODEL, tn), BF16),
                        pltpu.VMEM((8 + tm, tn), F32)],
        compiler_params=_cparams(("arbitrary", "arbitrary")),
        name="ffn_up",
    )(h2, h2, w_up, w_up, w_up, w_up, w_up, conv_w, conv_b.reshape(1, D_FF), sffn_t)


def _pad_lanes(v):
    return jnp.pad(v, ((0, 0), (0, LANES - v.shape[-1])))


def kernel(x_prompt, x_sample, c_prompt, c_sample, state_pool, state_conv, state_ssm, state_ffn, norm1, norm2, w_ada, b_ada, w_in, pool_w, pool_scale, conv_w, conv_b, dt_bias, a_log, d_skip, ssd_norm, w_out, w_up, ffn_conv_w, ffn_conv_b, w_down, norm_f):
    x = jnp.concatenate([x_prompt.reshape(TP, D_MODEL),
                         jnp.swapaxes(x_sample, 0, 1).reshape(TS, D_MODEL)], axis=0)
    c_all = jnp.concatenate([c_sample, c_prompt, jnp.zeros((4, D_MODEL), F32)], axis=0)
    mods, modp = _ada(c_all, w_ada, b_ada)
    modp = modp.reshape(DEPTH, BATCH, 1, 6 * D_MODEL)

    spool_t = jnp.swapaxes(state_pool, 1, 2)
    sconv_t = jnp.swapaxes(state_conv, 1, 2)
    sffn_t = jnp.swapaxes(state_ffn, 1, 2)

    w_in_t = jnp.swapaxes(w_in, 1, 2)

    d_skip_e = jnp.repeat(d_skip, SSD_HEAD_DIM, axis=-1)

    outs = {k: [] for k in ("pp", "pc", "ps", "pf", "sp", "sc", "sf")}
    h = _norm0(x, norm1[0], modp, mods, 0)
    nssm_stack = None
    for l in range(DEPTH):
        lw = {"pool_w": pool_w[l], "pool_scale": pool_scale[l].reshape(1, POOL_WIDTH),
              "conv_w": conv_w[l], "conv_b": conv_b[l].reshape(1, XBC_WIDTH),
              "dt_bias": _pad_lanes(dt_bias[l].reshape(1, SSD_HEADS)),
              "a_log": _pad_lanes(a_log[l].reshape(1, SSD_HEADS)),
              "d_skip_e": d_skip_e[l].reshape(1, SSD_WIDTH),
              "ssd_norm": ssd_norm[l].reshape(1, SSD_WIDTH)}
        proj, dtp = _inproj(h, w_in_t, l)
        mix_p, st_p, w_out_b = _mix_prompt(proj, dtp, lw, w_out, l)
        mix_s, nconv_s, nssm_stack = _mix_sample(proj, dtp, sconv_t, spool_t, state_ssm, lw, l, nssm_stack)
        x1, h2, w_down_b = _resid_proj(mix_p, mix_s.reshape(TS, D_MODEL), w_out_b, x, norm2[l], modp, mods, l, 2,
                                       (l, 3, 4), side_w=w_down)
        act_p, act_s, nffn_p, nffn_s = _ffn_up(h2, w_up, ffn_conv_w[l], ffn_conv_b[l], sffn_t, l)
        if l + 1 < DEPTH:
            x, h = _resid_proj(act_p, act_s, w_down_b, x1, norm1[l + 1], modp, mods, l, 5, (l + 1, 0, 1))
        else:
            y_p, y_s = _resid_proj(act_p, act_s, w_down_b, x1, norm_f, modp, mods, l, 5, None)

        ends = [(b + 1) * SEQ for b in range(BATCH)]
        outs["pp"].append(jnp.stack([lax.slice(proj, (e - POOL_BUF, 0), (e, POOL_WIDTH)) for e in ends]))
        outs["pc"].append(jnp.stack([lax.slice(proj, (e - (SSD_CONV - 1), POOL_WIDTH + SSD_WIDTH), (e, MAIN_WIDTH))
                                     for e in ends]))
        outs["ps"].append(jnp.transpose(st_p.reshape(BATCH, D_STATE, SSD_HEADS, SSD_HEAD_DIM), (0, 2, 3, 1)))
        outs["pf"].append(nffn_p)
        u_s = jnp.swapaxes(lax.slice(proj, (TP, 0), (T, POOL_WIDTH)).reshape(DEC_SEQ, DEC_BATCH, POOL_WIDTH), 0, 1)
        outs["sp"].append(jnp.concatenate([state_pool[l][:, DEC_SEQ:], u_s], axis=1))
        outs["sc"].append(jnp.swapaxes(nconv_s, 0, 1))
        outs["sf"].append(jnp.swapaxes(nffn_s, 0, 1))

    y_prompt = y_p.reshape(BATCH, SEQ, D_MODEL)
    y_sample = jnp.swapaxes(y_s.reshape(DEC_SEQ, DEC_BATCH, D_MODEL), 0, 1)
    return (y_prompt, y_sample,
            jnp.stack(outs["pp"]), jnp.stack(outs["pc"]), jnp.stack(outs["ps"]), jnp.stack(outs["pf"]),
            jnp.stack(outs["sp"]), jnp.stack(outs["sc"]), nssm_stack, jnp.stack(outs["sf"]))
```

```python
import functools

import jax
import jax.numpy as jnp
from jax import lax
from jax.experimental import pallas as pl
from jax.experimental.pallas import tpu as pltpu

F32 = jnp.float32
BF16 = jnp.bfloat16

D_MODEL = 2048
BATCH = 4
SEQ = 2048
DEPTH = 4
DEC_BATCH = 128
DEC_SEQ = 4
PAST_LEN = 16384

POOL_WIDTH = 512
POOL_WINDOWS = (2, 4, 8, 16)
POOL_GROUP_DIM = 128
POOL_BUF = 15
SSD_WIDTH = 1536
SSD_HEAD_DIM = 64
SSD_HEADS = 24
SSD_GROUPS = 4
SSD_HPG = 6
D_STATE = 128
SSD_CONV = 4
XBC_WIDTH = 2560
MAIN_WIDTH = POOL_WIDTH + SSD_WIDTH + XBC_WIDTH
D_FF = 5504
FFN_CONV = 3
EPS = 1e-6

TP = BATCH * SEQ
TS = DEC_SEQ * DEC_BATCH
T = TP + TS
GROUP_COLS = SSD_HPG * SSD_HEAD_DIM
LANES = 128
BF16_SUBLANES = 16
CHUNK = 128

VMEM_LIMIT = 56 * 1024 * 1024


def _silu(x):
    return x * jax.nn.sigmoid(x)


def _softplus(x):
    return jnp.maximum(x, 0.0) + jnp.log1p(jnp.exp(-jnp.abs(x)))


def _cparams(sem):
    return pltpu.CompilerParams(dimension_semantics=sem, vmem_limit_bytes=VMEM_LIMIT)


def _rowmod(p_ref, s_ref, is_sample, tm):
    s = s_ref[...]
    if tm > DEC_BATCH:
        s = jnp.concatenate([s] * (tm // DEC_BATCH), axis=0)
    return jnp.where(is_sample, s, p_ref[...])


def _rms_mod(x, gain, sc, sh):
    ms = jnp.mean(x * x, axis=-1, keepdims=True)
    y = x * lax.rsqrt(ms + EPS) * gain
    return y * (1.0 + sc) + sh


def _ada_kernel(c_ref, w_ref, b_ref, os_ref, op_ref):
    a = _silu(c_ref[...]).astype(BF16)
    m = jnp.dot(a, w_ref[...].astype(BF16), preferred_element_type=F32) + b_ref[...]
    os_ref[...] = m[:DEC_BATCH]
    op_ref[...] = m[DEC_BATCH:DEC_BATCH + BATCH]


def _ada(c_all, w_ada, b_ada):
    tn = 1024
    rows = c_all.shape[0]
    return pl.pallas_call(
        _ada_kernel,
        grid=(DEPTH, 6 * D_MODEL // tn),
        in_specs=[
            pl.BlockSpec((rows, D_MODEL), lambda l, j: (0, 0)),
            pl.BlockSpec((None, D_MODEL, tn), lambda l, j: (l, 0, j)),
            pl.BlockSpec((None, 1, tn), lambda l, j: (l, 0, j)),
        ],
        out_specs=[
            pl.BlockSpec((None, DEC_BATCH, tn), lambda l, j: (l, 0, j)),
            pl.BlockSpec((None, BATCH, tn), lambda l, j: (l, 0, j)),
        ],
        out_shape=[
            jax.ShapeDtypeStruct((DEPTH, DEC_BATCH, 6 * D_MODEL), F32),
            jax.ShapeDtypeStruct((DEPTH, BATCH, 6 * D_MODEL), F32),
        ],
        compiler_params=_cparams(("arbitrary", "arbitrary")),
        name="ada",
    )(c_all, w_ada, b_ada.reshape(DEPTH, 1, 6 * D_MODEL))


def _mod_specs(l, k, tm, row_of):
    tpb = SEQ // tm

    def pmap(*ids):
        return (l, jnp.minimum(row_of(*ids) // tpb, BATCH - 1), 0, k)

    def smap(*ids):
        return (l, 0, k)

    return (pl.BlockSpec((None, None, 1, D_MODEL), pmap), pl.BlockSpec((None, DEC_BATCH, D_MODEL), smap))


def _norm_kernel(xp_ref, xs_ref, g_ref, scp, scs, shp, shs, o_ref, xo_ref, *, npt, tm):
    is_s = pl.program_id(0) >= npt
    x = jnp.where(is_s, xs_ref[...], xp_ref[...])
    sc = _rowmod(scp, scs, is_s, tm)
    sh = _rowmod(shp, shs, is_s, tm)
    o_ref[...] = _rms_mod(x, g_ref[...], sc, sh).astype(o_ref.dtype)
    xo_ref[...] = x


def _norm0(x_p, x_s, gain, modp, mods, l):
    tm = TS
    npt = TP // tm
    row = lambda i: i
    scp, scs = _mod_specs(l, 1, tm, row)
    shp, shs = _mod_specs(l, 0, tm, row)
    return pl.pallas_call(
        functools.partial(_norm_kernel, npt=npt, tm=tm),
        grid=(T // tm,),
        in_specs=[pl.BlockSpec((tm, D_MODEL), lambda i: (jnp.minimum(i, npt - 1), 0)),
                  pl.BlockSpec((tm, D_MODEL), lambda i: (jnp.maximum(i - npt, 0), 0)),
                  pl.BlockSpec((1, D_MODEL), lambda i: (0, 0)),
                  scp, scs, shp, shs],
        out_specs=[pl.BlockSpec((tm, D_MODEL), lambda i: (i, 0)),
                   pl.BlockSpec((tm, D_MODEL), lambda i: (i, 0))],
        out_shape=[jax.ShapeDtypeStruct((T, D_MODEL), BF16),
                   jax.ShapeDtypeStruct((T, D_MODEL), F32)],
        compiler_params=_cparams(("arbitrary",)),
        name="norm0",
    )(x_p, x_s, gain.reshape(1, D_MODEL), modp, mods, modp, mods)


_NT = (((1,), (1,)), ((), ()))


def _inproj_kernel(h_ref, w_ref, wdt_ref, o_ref, dt_ref, wb, wdtb):
    j = pl.program_id(0)
    m = pl.program_id(1)

    @pl.when(m == 0)
    def _():
        wb[...] = w_ref[...].astype(BF16)

    h = h_ref[...]
    o_ref[...] = lax.dot_general(h, wb[...], _NT, preferred_element_type=F32)

    @pl.when(j == 0)
    def _():
        @pl.when(m == 0)
        def _():
            row = lax.broadcasted_iota(jnp.int32, (LANES, 1), 0)
            wdtb[...] = jnp.where(row < SSD_HEADS, wdt_ref[...], 0.0).astype(BF16)

        dt_ref[...] = lax.dot_general(h, wdtb[...], _NT, preferred_element_type=F32)

    @pl.when(j != 0)
    def _():
        dt_ref[...] = jnp.zeros(dt_ref.shape, F32)


def _inproj(h, w_in_t, l):
    tm, tn = 512, 1536
    nm = T // tm
    return pl.pallas_call(
        _inproj_kernel,
        grid=(MAIN_WIDTH // tn, nm),
        in_specs=[pl.BlockSpec((tm, D_MODEL), lambda j, m: (m, 0)),
                  pl.BlockSpec((None, tn, D_MODEL), lambda j, m: (l, j, 0)),
                  pl.BlockSpec((None, LANES, D_MODEL), lambda j, m: (l, MAIN_WIDTH // LANES, 0))],
        out_specs=[pl.BlockSpec((tm, tn), lambda j, m: (m, j)),
                   pl.BlockSpec((tm, LANES), lambda j, m: (jnp.where(j == 0, m, nm), 0))],
        out_shape=[jax.ShapeDtypeStruct((T, MAIN_WIDTH), F32),
                   jax.ShapeDtypeStruct((T + tm, LANES), F32)],
        scratch_shapes=[pltpu.VMEM((tn, D_MODEL), BF16), pltpu.VMEM((LANES, D_MODEL), BF16)],
        compiler_params=_cparams(("arbitrary", "arbitrary")),
        name="inproj",
    )(h, w_in_t, w_in_t)


def _expand_heads(v, rows):
    lo = lax.broadcasted_iota(jnp.int32, (rows, LANES), 1) < SSD_HEAD_DIM
    parts = []
    for j in range(SSD_HEADS // 2):
        a = jnp.broadcast_to(v[:, 2 * j:2 * j + 1], (rows, LANES))
        b = jnp.broadcast_to(v[:, 2 * j + 1:2 * j + 2], (rows, LANES))
        parts.append(jnp.where(lo, a, b))
    return jnp.concatenate(parts, axis=1)


def _split3(x):
    a = x.astype(BF16)
    r = x - a.astype(F32)
    b = r.astype(BF16)
    c = (r - b.astype(F32)).astype(BF16)
    return a, b, c


def _gated_norm(y, z, gain):
    y = y * _silu(z)
    ms = jnp.mean(y * y, axis=-1, keepdims=True)
    return y * lax.rsqrt(ms + EPS) * gain


PROMPT_CHUNKS_PER_STEP = 2


def _mixp_kernel(proj_ref, dt_ref, poolw_ref, pscale_ref, cw_ref, cb_ref, dtb_ref, alog_ref, dskip_ref, ng_ref, hc_ref,
                 wo_ref, mix_ref, st_ref, wob_ref, xpad, upad, hT):
    c = pl.program_id(1)

    @pl.when(c == 0)
    def _():
        xpad[0:8, :] = jnp.zeros((8, XBC_WIDTH), F32)
        upad[0:16, :] = jnp.zeros((16, POOL_WIDTH), F32)
        hT[...] = jnp.zeros(hT.shape, F32)

    wob_ref[...] = wo_ref[...].astype(BF16)

    for sub in range(PROMPT_CHUNKS_PER_STEP):
        _mixp_chunk(c * PROMPT_CHUNKS_PER_STEP + sub, slice(sub * CHUNK, (sub + 1) * CHUNK),
                    proj_ref, dt_ref, poolw_ref, pscale_ref, cw_ref, cb_ref, dtb_ref, alog_ref, dskip_ref, ng_ref,
                    hc_ref, mix_ref, xpad, upad, hT)

    @pl.when(c == pl.num_programs(1) - 1)
    def _():
        st_ref[...] = hT[...]


def _mixp_chunk(chunk, r, proj_ref, dt_ref, poolw_ref, pscale_ref, cw_ref, cb_ref, dtb_ref, alog_ref, dskip_ref, ng_ref,
                hc_ref, mix_ref, xpad, upad, hT):
    Q = CHUNK

    u = proj_ref[r, 0:POOL_WIDTH]
    upad[16:16 + Q, :] = u
    pos = chunk * Q + lax.broadcasted_iota(jnp.int32, (Q, 1), 0)
    outs = []
    for g, w in enumerate(POOL_WINDOWS):
        lo = g * POOL_GROUP_DIM
        ug = u[:, lo:lo + POOL_GROUP_DIM]
        s = ug
        for k in range(1, w):
            s = s + upad[16 - k:16 - k + Q, lo:lo + POOL_GROUP_DIM]
        cnt = jnp.minimum(pos + 1, w).astype(F32)
        d = s / cnt - ug
        outs.append(jnp.dot(d.astype(BF16), poolw_ref[g].astype(BF16), preferred_element_type=F32))
    mix_ref[r, 0:POOL_WIDTH] = (jnp.concatenate(outs, axis=1) * pscale_ref[...]).astype(mix_ref.dtype)
    upad[0:16, :] = upad[Q:Q + 16, :]

    xbc = proj_ref[r, POOL_WIDTH + SSD_WIDTH:MAIN_WIDTH]
    xpad[8:8 + Q, :] = xbc
    acc = cb_ref[...] + cw_ref[3:4, :] * xbc
    for i in range(SSD_CONV - 1):
        acc = acc + cw_ref[i:i + 1, :] * xpad[5 + i:5 + i + Q, :]
    xpad[0:8, :] = xpad[Q:Q + 8, :]
    xc = _silu(acc)
    xs = xc[:, 0:SSD_WIDTH]
    Bb = xc[:, SSD_WIDTH:SSD_WIDTH + SSD_GROUPS * D_STATE].astype(BF16)
    Cb = xc[:, SSD_WIDTH + SSD_GROUPS * D_STATE:XBC_WIDTH].astype(BF16)

    dt = _softplus(dt_ref[r, :] + dtb_ref[...])
    dA = dt * (-jnp.exp(alog_ref[...]))
    row = lax.broadcasted_iota(jnp.int32, (Q, Q), 0)
    col = lax.broadcasted_iota(jnp.int32, (Q, Q), 1)
    tri = row >= col
    tri_lo = jnp.where(tri, 1.0, 0.0).astype(BF16)
    tri_up = jnp.where(row <= col, 1.0, 0.0).astype(BF16)
    pieces = _split3(dA)
    cs = sum(jnp.dot(tri_lo, p, preferred_element_type=F32) for p in pieces)
    csT = sum(lax.dot_general(p, tri_up, (((0,), (0,)), ((), ())), preferred_element_type=F32)
              for p in pieces)
    cs_last = cs[Q - 1:Q, :]
    stacked = jnp.concatenate([dt, jnp.exp(cs), jnp.exp(cs_last - cs)], axis=0)
    head_cols = hc_ref[...]
    expanded = sum(jnp.dot(p, head_cols, preferred_element_type=F32) for p in _split3(stacked))
    expcs_e = expanded[Q:2 * Q]
    dte_e = expanded[2 * Q:3 * Q]
    xdt = xs * expanded[0:Q]
    xdt_b = xdt.astype(BF16)
    lo_half = lax.broadcasted_iota(jnp.int32, (Q, LANES), 1) < SSD_HEAD_DIM
    xw_b = (xdt * dte_e).astype(BF16)
    decay_e = expcs_e[Q - 1:Q, :]
    h_prev = hT[...]
    h_prev_b = h_prev.astype(BF16)

    ys = []
    for g in range(SSD_GROUPS):
        gs = slice(g * GROUP_COLS, (g + 1) * GROUP_COLS)
        Cg = Cb[:, g * D_STATE:(g + 1) * D_STATE]
        Bg = Bb[:, g * D_STATE:(g + 1) * D_STATE]
        CB = lax.dot_general(Cg, Bg, (((1,), (1,)), ((), ())), preferred_element_type=F32)
        y_off = jnp.dot(Cg, h_prev_b[:, gs], preferred_element_type=F32)
        ST = lax.dot_general(Bg, xw_b[:, gs], (((0,), (0,)), ((), ())), preferred_element_type=F32)
        hT[:, gs] = h_prev[:, gs] * decay_e[:, gs] + ST
        parts = []
        for j in range(SSD_HPG // 2):
            h1 = SSD_HPG * g + 2 * j
            ps = slice(h1 * SSD_HEAD_DIM, h1 * SSD_HEAD_DIM + LANES)
            W = []
            for h in (h1, h1 + 1):
                seg = cs[:, h:h + 1] - csT[h:h + 1, :]
                W.append((CB * jnp.exp(jnp.where(tri, seg, -jnp.inf))).astype(BF16))
            parts.append(jnp.where(lo_half,
                                   jnp.dot(W[0], xdt_b[:, ps], preferred_element_type=F32),
                                   jnp.dot(W[1], xdt_b[:, ps], preferred_element_type=F32)))
        ys.append(jnp.concatenate(parts, axis=1) + y_off * expcs_e[:, gs])
    y = jnp.concatenate(ys, axis=1) + dskip_ref[...] * xs
    z = proj_ref[r, POOL_WIDTH:POOL_WIDTH + SSD_WIDTH]
    mix_ref[r, POOL_WIDTH:D_MODEL] = _gated_norm(y, z, ng_ref[...]).astype(mix_ref.dtype)


def _mix_prompt(proj, dtp, lw, w_out, l):
    rows = CHUNK * PROMPT_CHUNKS_PER_STEP
    nc = SEQ // rows
    wrows = D_MODEL // (BATCH * nc)
    full = lambda shape: pl.BlockSpec(shape, lambda b, c: (0,) * len(shape))
    return pl.pallas_call(
        _mixp_kernel,
        grid=(BATCH, nc),
        in_specs=[pl.BlockSpec((rows, MAIN_WIDTH), lambda b, c: (b * nc + c, 0)),
                  pl.BlockSpec((rows, LANES), lambda b, c: (b * nc + c, 0)),
                  full((len(POOL_WINDOWS), POOL_GROUP_DIM, POOL_GROUP_DIM)),
                  full((1, POOL_WIDTH)),
                  full((SSD_CONV, XBC_WIDTH)),
                  full((1, XBC_WIDTH)),
                  full((1, LANES)), full((1, LANES)),
                  full((1, SSD_WIDTH)), full((1, SSD_WIDTH)),
                  full((LANES, SSD_WIDTH)),
                  pl.BlockSpec((None, wrows, D_MODEL), lambda b, c: (l, b * nc + c, 0))],
        out_specs=[pl.BlockSpec((rows, D_MODEL), lambda b, c: (b * nc + c, 0)),
                   pl.BlockSpec((None, D_STATE, SSD_WIDTH), lambda b, c: (b, 0, 0)),
                   pl.BlockSpec((wrows, D_MODEL), lambda b, c: (b * nc + c, 0))],
        out_shape=[jax.ShapeDtypeStruct((TP, D_MODEL), BF16),
                   jax.ShapeDtypeStruct((BATCH, D_STATE, SSD_WIDTH), F32),
                   jax.ShapeDtypeStruct((D_MODEL, D_MODEL), BF16)],
        scratch_shapes=[pltpu.VMEM((8 + CHUNK, XBC_WIDTH), F32),
                        pltpu.VMEM((16 + CHUNK, POOL_WIDTH), F32),
                        pltpu.VMEM((D_STATE, SSD_WIDTH), F32)],
        compiler_params=_cparams(("arbitrary", "arbitrary")),
        name="mix_prompt",
    )(proj, dtp, lw["pool_w"], lw["pool_scale"], lw["conv_w"], lw["conv_b"], lw["dt_bias"], lw["a_log"],
      lw["d_skip_e"], lw["ssd_norm"], _head_cols(), w_out)


def _head_cols():
    head_of_col = jnp.arange(SSD_WIDTH, dtype=jnp.int32) // SSD_HEAD_DIM
    return (head_of_col[None, :] == jnp.arange(LANES, dtype=jnp.int32)[:, None]).astype(BF16)


SAMPLE_BT = 8


def _mixs_kernel(*refs, aliased):
    if aliased:
        refs = refs[1:]
    (p0, p1, p2, p3, d0, d1, d2, d3, sconv_ref, spool_ref, sssm_ref,
     poolw_ref, pscale_ref, cw_ref, cb_ref, dtb_ref, alog_ref, dskip_ref, ng_ref,
     mix_ref, nconv_ref, nssm_ref, c_scr, b_scr, xw_scr, yoff_scr) = refs
    bt = SAMPLE_BT
    P = (p0, p1, p2, p3)
    Dt = (d0, d1, d2, d3)
    L = DEC_SEQ

    fullp = [spool_ref[k] for k in range(POOL_BUF)] + [P[t][:, 0:POOL_WIDTH] for t in range(L)]
    for g, w in enumerate(POOL_WINDOWS):
        gl = slice(g * POOL_GROUP_DIM, (g + 1) * POOL_GROUP_DIM)
        ds = []
        for t in range(L):
            s = fullp[POOL_BUF + t][:, gl]
            for k in range(1, w):
                s = s + fullp[POOL_BUF + t - k][:, gl]
            cnt = float(min(PAST_LEN + t + 1, w))
            ds.append(s / cnt - fullp[POOL_BUF + t][:, gl])
        o = jnp.dot(jnp.concatenate(ds, axis=0).astype(BF16), poolw_ref[g].astype(BF16),
                    preferred_element_type=F32) * pscale_ref[:, gl]
        for t in range(L):
            mix_ref[t, :, gl] = o[t * bt:(t + 1) * bt]

    fullc = [sconv_ref[k] for k in range(SSD_CONV - 1)] + [P[t][:, POOL_WIDTH + SSD_WIDTH:MAIN_WIDTH] for t in range(L)]
    for k in range(SSD_CONV - 1):
        nconv_ref[k] = fullc[L + k]
    xs, Bm, Cm = [], [], []
    for t in range(L):
        acc = cb_ref[...]
        for i in range(SSD_CONV):
            acc = acc + cw_ref[i:i + 1, :] * fullc[t + i]
        xc = _silu(acc)
        xs.append(xc[:, 0:SSD_WIDTH])
        Bm.append(xc[:, SSD_WIDTH:SSD_WIDTH + SSD_GROUPS * D_STATE])
        Cm.append(xc[:, SSD_WIDTH + SSD_GROUPS * D_STATE:XBC_WIDTH])

    A = -jnp.exp(alog_ref[...])
    dts = [_softplus(Dt[t][...] + dtb_ref[...]) for t in range(L)]
    cs = []
    for t in range(L):
        dA = dts[t] * A
        cs.append(dA if t == 0 else cs[t - 1] + dA)
    dec24 = jnp.exp(cs[L - 1])
    expcs_e = [_expand_heads(jnp.exp(cs[t]), bt) for t in range(L)]
    xdt = [xs[t] * _expand_heads(dts[t], bt) for t in range(L)]

    ydiag = []
    for t in range(L):
        acc = None
        for s in range(t + 1):
            cb_parts = []
            for g in range(SSD_GROUPS):
                sl = slice(g * D_STATE, (g + 1) * D_STATE)
                r = jnp.sum(Cm[t][:, sl] * Bm[s][:, sl], axis=-1, keepdims=True)
                cb_parts.append(jnp.broadcast_to(r, (bt, GROUP_COLS)))
            coef = jnp.concatenate(cb_parts, axis=1)
            if s < t:
                coef = coef * _expand_heads(jnp.exp(cs[t] - cs[s]), bt)
            term = coef * xdt[s]
            acc = term if acc is None else acc + term
        ydiag.append(acc)

    for t in range(L):
        c_scr[t] = Cm[t]
        b_scr[t] = Bm[t]
        xw_scr[t] = xdt[t] * _expand_heads(jnp.exp(cs[L - 1] - cs[t]), bt)
    zpad_s = jnp.zeros((8 - L, D_STATE), F32)
    zpad_x = jnp.zeros((8 - L, GROUP_COLS), F32)
    for b in range(bt):
        for g in range(SSD_GROUPS):
            sl = slice(g * D_STATE, (g + 1) * D_STATE)
            gs = slice(g * GROUP_COLS, (g + 1) * GROUP_COLS)
            C8 = jnp.concatenate([c_scr[t, b:b + 1, sl] for t in range(L)] + [zpad_s], axis=0)
            B8 = jnp.concatenate([b_scr[t, b:b + 1, sl] for t in range(L)] + [zpad_s], axis=0)
            X8 = jnp.concatenate([xw_scr[t, b:b + 1, gs] for t in range(L)] + [zpad_x], axis=0)
            h0 = sssm_ref[b, SSD_HPG * g:SSD_HPG * (g + 1)].reshape(GROUP_COLS, D_STATE)
            GT = lax.dot_general(C8.astype(BF16), h0.astype(BF16), (((1,), (1,)), ((), ())),
                                 preferred_element_type=F32)
            U = lax.dot_general(X8, B8, (((0,), (0,)), ((), ())), preferred_element_type=F32)
            for r in range(SSD_HPG):
                h = SSD_HPG * g + r
                dec = jnp.broadcast_to(dec24[b:b + 1, h:h + 1], (SSD_HEAD_DIM, D_STATE))
                rs = slice(r * SSD_HEAD_DIM, (r + 1) * SSD_HEAD_DIM)
                nssm_ref[b, h] = h0[rs] * dec + U[rs]
            for t in range(L):
                yoff_scr[t, b:b + 1, gs] = GT[t:t + 1, :]

    for t in range(L):
        y = ydiag[t] + expcs_e[t] * yoff_scr[t] + dskip_ref[...] * xs[t]
        z = P[t][:, POOL_WIDTH:POOL_WIDTH + SSD_WIDTH]
        mix_ref[t, :, POOL_WIDTH:D_MODEL] = _gated_norm(y, z, ng_ref[...])


def _mix_sample(proj, dtp, sconv_t, spool_t, sssm, lw, l, nssm_stack):
    bt = SAMPLE_BT
    nb = DEC_BATCH // bt
    base = TP // bt
    aliased = nssm_stack is not None

    def pspec(t, width):
        return pl.BlockSpec((bt, width), lambda i, t=t: (base + t * nb + i, 0))

    full = lambda shape: pl.BlockSpec(shape, lambda i: (0,) * len(shape))
    return pl.pallas_call(
        functools.partial(_mixs_kernel, aliased=aliased),
        grid=(nb,),
        input_output_aliases={0: 2} if aliased else {},
        in_specs=([pl.BlockSpec(memory_space=pl.ANY)] if aliased else [])
        + [pspec(t, MAIN_WIDTH) for t in range(DEC_SEQ)] + [pspec(t, LANES) for t in range(DEC_SEQ)] + [
            pl.BlockSpec((None, SSD_CONV - 1, bt, XBC_WIDTH), lambda i: (l, 0, i, 0)),
            pl.BlockSpec((None, POOL_BUF, bt, POOL_WIDTH), lambda i: (l, 0, i, 0)),
            pl.BlockSpec((None, bt, SSD_HEADS, SSD_HEAD_DIM, D_STATE), lambda i: (l, i, 0, 0, 0)),
            full((len(POOL_WINDOWS), POOL_GROUP_DIM, POOL_GROUP_DIM)),
            full((1, POOL_WIDTH)),
            full((SSD_CONV, XBC_WIDTH)),
            full((1, XBC_WIDTH)),
            full((1, LANES)), full((1, LANES)),
            full((1, SSD_WIDTH)), full((1, SSD_WIDTH))],
        out_specs=[pl.BlockSpec((DEC_SEQ, bt, D_MODEL), lambda i: (0, i, 0)),
                   pl.BlockSpec((SSD_CONV - 1, bt, XBC_WIDTH), lambda i: (0, i, 0)),
                   pl.BlockSpec((None, bt, SSD_HEADS, SSD_HEAD_DIM, D_STATE), lambda i: (l, i, 0, 0, 0))],
        out_shape=[jax.ShapeDtypeStruct((DEC_SEQ, DEC_BATCH, D_MODEL), F32),
                   jax.ShapeDtypeStruct((SSD_CONV - 1, DEC_BATCH, XBC_WIDTH), F32),
                   jax.ShapeDtypeStruct((DEPTH, DEC_BATCH, SSD_HEADS, SSD_HEAD_DIM, D_STATE), F32)],
        scratch_shapes=[pltpu.VMEM((DEC_SEQ, bt, SSD_GROUPS * D_STATE), F32),
                        pltpu.VMEM((DEC_SEQ, bt, SSD_GROUPS * D_STATE), F32),
                        pltpu.VMEM((DEC_SEQ, bt, SSD_WIDTH), F32),
                        pltpu.VMEM((DEC_SEQ, bt, SSD_WIDTH), F32)],
        compiler_params=_cparams(("arbitrary",)),
        name="mix_sample",
    )(*([nssm_stack] if aliased else []),
      proj, proj, proj, proj, dtp, dtp, dtp, dtp, sconv_t, spool_t, sssm,
      lw["pool_w"], lw["pool_scale"], lw["conv_w"], lw["conv_b"], lw["dt_bias"], lw["a_log"],
      lw["d_skip_e"], lw["ssd_norm"])


RESID_PIECE_ROWS = 256


def _row_pieces(tm):
    return [slice(r0, r0 + RESID_PIECE_ROWS) for r0 in range(0, tm, RESID_PIECE_ROWS)]


def _resid_epilogue(acc, x_ref, r, gp, gs, ng_ref, mods, is_s):
    rows = r.stop - r.start
    xn = x_ref[r, :] + _rowmod(gp, gs, is_s, rows) * acc
    if mods is None:
        ms = jnp.mean(xn * xn, axis=-1, keepdims=True)
        return xn, xn * lax.rsqrt(ms + EPS) * ng_ref[...]
    scp, scs, shp, shs = mods
    return xn, _rms_mod(xn, ng_ref[...], _rowmod(scp, scs, is_s, rows), _rowmod(shp, shs, is_s, rows))


def _resid_kernel(ap_ref, as_ref, w_ref, x_ref, gp, gs, ng_ref, scp, scs, shp, shs, *rest, npt, tm):
    if len(rest) == 4:
        side_in, xo_ref, ho_ref, side_out = rest
        side_out[...] = side_in[...].astype(BF16)
    else:
        xo_ref, ho_ref = rest
    is_s = pl.program_id(0) >= npt
    for r in _row_pieces(tm):
        a = jnp.where(is_s, as_ref[r, :].astype(BF16), ap_ref[r, :])
        acc = jnp.dot(a, w_ref[...], preferred_element_type=F32)
        xn, hn = _resid_epilogue(acc, x_ref, r, gp, gs, ng_ref, (scp, scs, shp, shs), is_s)
        xo_ref[r, :] = xn
        ho_ref[r, :] = hn.astype(ho_ref.dtype)


def _final_kernel(ap_ref, as_ref, w_ref, x_ref, gp, gs, ng_ref, yp_ref, ys_ref, *, npt, tm):
    is_s = pl.program_id(0) >= npt
    a = jnp.where(is_s, as_ref[...].astype(BF16), ap_ref[...])
    acc = jnp.dot(a, w_ref[...], preferred_element_type=F32)
    _, yn = _resid_epilogue(acc, x_ref, slice(0, tm), gp, gs, ng_ref, None, is_s)

    @pl.when(jnp.logical_not(is_s))
    def _():
        yp_ref[...] = yn

    @pl.when(is_s)
    def _():
        ys_ref[...] = yn


def _resid_proj(a_p, a_s, w, x, gain, modp, mods, l, gate_k, next_mod, side_w=None):
    kdim = w.shape[-2]
    tm = 512 if kdim <= D_MODEL else 256
    npt = TP // tm
    row = lambda i: i
    if w.ndim == 3:
        w_spec = pl.BlockSpec((None, kdim, D_MODEL), lambda i: (l, 0, 0), pipeline_mode=pl.Buffered(1))
    else:
        w_spec = pl.BlockSpec((kdim, D_MODEL), lambda i: (0, 0), pipeline_mode=pl.Buffered(1))
    gp, gs = _mod_specs(l, gate_k, tm, row)
    kern = _resid_kernel
    in_specs = [pl.BlockSpec((tm, kdim), lambda i: (jnp.minimum(i, npt - 1), 0)),
                pl.BlockSpec((tm, kdim), lambda i: (jnp.maximum(i - npt, 0), 0)),
                w_spec,
                pl.BlockSpec((tm, D_MODEL), lambda i: (i, 0)),
                gp, gs,
                pl.BlockSpec((1, D_MODEL), lambda i: (0, 0))]
    args = [a_p, a_s, w, x, modp, mods, gain.reshape(1, D_MODEL)]
    if next_mod is None:
        kern = _final_kernel
        out_specs = [pl.BlockSpec((tm, D_MODEL), lambda i: (jnp.minimum(i, npt - 1), 0)),
                     pl.BlockSpec((tm, D_MODEL), lambda i: (jnp.maximum(i - npt, 0), 0))]
        out_shape = [jax.ShapeDtypeStruct((TP, D_MODEL), F32), jax.ShapeDtypeStruct((TS, D_MODEL), F32)]
    else:
        nl, shift_k, scale_k = next_mod
        scp, scs = _mod_specs(nl, scale_k, tm, row)
        shp, shs = _mod_specs(nl, shift_k, tm, row)
        in_specs += [scp, scs, shp, shs]
        args += [modp, mods, modp, mods]
        out_specs = [pl.BlockSpec((tm, D_MODEL), lambda i: (i, 0)),
                     pl.BlockSpec((tm, D_MODEL), lambda i: (i, 0))]
        out_shape = [jax.ShapeDtypeStruct((T, D_MODEL), F32), jax.ShapeDtypeStruct((T, D_MODEL), BF16)]
        if side_w is not None:
            srows = side_w.shape[1]
            blk = BF16_SUBLANES * pl.cdiv(srows, BF16_SUBLANES * (T // tm))
            in_specs.append(pl.BlockSpec((None, blk, D_MODEL), lambda i: (l, i, 0)))
            args.append(side_w)
            out_specs.append(pl.BlockSpec((blk, D_MODEL), lambda i: (i, 0)))
            out_shape.append(jax.ShapeDtypeStruct((srows, D_MODEL), BF16))
    return pl.pallas_call(
        functools.partial(kern, npt=npt, tm=tm),
        grid=(T // tm,),
        in_specs=in_specs,
        out_specs=out_specs,
        out_shape=out_shape,
        compiler_params=_cparams(("arbitrary",)),
        name="resid_proj",
    )(*args)


def _ffn_gate(gate, val, gm1, gm2, cw_ref, cb_ref, cs):
    gc = cb_ref[:, cs] + cw_ref[2:3, cs] * gate + cw_ref[1:2, cs] * gm1 + cw_ref[0:1, cs] * gm2
    return _silu(gc) * val


UP_PIECE_ROWS = 512


def _up_kernel(h_ref, hs_ref, wg_ref, wv0, wv1, wv2, wv3, cw_ref, cb_ref, sf_ref,
               act_ref, acts_ref, nfp_ref, nfs_ref, wgb, wvb, gpad, *, tpb, tm, tn):
    m = pl.program_id(1)

    @pl.when(m == 0)
    def _():
        wgb[...] = wg_ref[...].astype(BF16)
        for k, r in enumerate((wv0, wv1, wv2, wv3)):
            wvb[:, k * LANES:(k + 1) * LANES] = r[...].astype(BF16)

    first = (m % tpb) == 0
    hw = tn // 2
    P = UP_PIECE_ROWS
    plan = [(r0, P, 0) for r0 in range(0, tm, P)] + [(r0, P, 1) for r0 in range(0, tm - P, P)]
    plan += [(tm - P, P // 2, 1), (tm - P // 2, P // 2, 1)]
    for r0, rows, half in plan:
        cs = slice(half * hw, (half + 1) * hw)
        hc = h_ref[r0:r0 + rows, :]
        gate = jnp.dot(hc, wgb[:, cs], preferred_element_type=F32)
        val = jnp.dot(hc, wvb[:, cs], preferred_element_type=F32)
        if r0 == 0:
            gpad[0:8, cs] = jnp.where(first, 0.0, gpad[0:8, cs])
        gpad[8 + r0:8 + r0 + rows, cs] = gate
        gm1 = gpad[7 + r0:7 + r0 + rows, cs]
        gm2 = gpad[6 + r0:6 + r0 + rows, cs]
        act_ref[r0:r0 + rows, cs] = _ffn_gate(gate, val, gm1, gm2, cw_ref, cb_ref, cs).astype(act_ref.dtype)
        if r0 + rows == tm:
            gpad[0:8, cs] = gpad[tm:tm + 8, cs]
            nfp_ref[:, cs] = gate[rows - (FFN_CONV - 1):rows]

    @pl.when(m == pl.num_programs(1) - 1)
    def _():
        hs = hs_ref[...]
        B1, B2 = DEC_BATCH, 2 * DEC_BATCH
        for half in range(2):
            cs = slice(half * hw, (half + 1) * hw)
            gate = jnp.dot(hs, wgb[:, cs], preferred_element_type=F32)
            val = jnp.dot(hs, wvb[:, cs], preferred_element_type=F32)
            gm1 = jnp.concatenate([sf_ref[1, :, cs], gate[:TS - B1]], axis=0)
            gm2 = jnp.concatenate([sf_ref[0, :, cs], sf_ref[1, :, cs], gate[:TS - B2]], axis=0)
            acts_ref[:, cs] = _ffn_gate(gate, val, gm1, gm2, cw_ref, cb_ref, cs).astype(acts_ref.dtype)
            nfs_ref[0, :, cs] = gate[(DEC_SEQ - 2) * B1:(DEC_SEQ - 1) * B1]
            nfs_ref[1, :, cs] = gate[(DEC_SEQ - 1) * B1:DEC_SEQ * B1]


def _ffn_up(h2, w_up, conv_w, conv_b, sffn_t, l):
    tm, tn = SEQ, 512
    assert tm % UP_PIECE_ROWS == 0 and DEC_SEQ == FFN_CONV + 1
    tpb = SEQ // tm
    nn = pl.cdiv(D_FF, tn)
    val0 = D_FF // LANES
    last = 2 * D_FF // LANES - 1

    def vspec(k):
        return pl.BlockSpec((None, D_MODEL, LANES),
                            lambda j, m: (l, 0, jnp.minimum(val0 + (tn // LANES) * j + k, last)))

    return pl.pallas_call(
        functools.partial(_up_kernel, tpb=tpb, tm=tm, tn=tn),
        grid=(nn, TP // tm),
        in_specs=[pl.BlockSpec((tm, D_MODEL), lambda j, m: (m, 0)),
                  pl.BlockSpec((TS, D_MODEL), lambda j, m: (TP // TS, 0), pipeline_mode=pl.Buffered(1)),
                  pl.BlockSpec((None, D_MODEL, tn), lambda j, m: (l, 0, j))]
                 + [vspec(k) for k in range(tn // LANES)]
                 + [pl.BlockSpec((FFN_CONV, tn), lambda j, m: (0, j)),
                    pl.BlockSpec((1, tn), lambda j, m: (0, j)),
                    pl.BlockSpec((None, FFN_CONV - 1, DEC_BATCH, tn), lambda j, m: (l, 0, 0, j))],
        out_specs=[pl.BlockSpec((tm, tn), lambda j, m: (m, j)),
                   pl.BlockSpec((TS, tn), lambda j, m: (0, j)),
                   pl.BlockSpec((None, FFN_CONV - 1, tn), lambda j, m: (m // tpb, 0, j)),
                   pl.BlockSpec((FFN_CONV - 1, DEC_BATCH, tn), lambda j, m: (0, 0, j))],
        out_shape=[jax.ShapeDtypeStruct((TP, D_FF), BF16),
                   jax.ShapeDtypeStruct((TS, D_FF), BF16),
                   jax.ShapeDtypeStruct((BATCH, FFN_CONV - 1, D_FF), F32),
                   jax.ShapeDtypeStruct((FFN_CONV - 1, DEC_BATCH, D_FF), F32)],
        scratch_shapes=[pltpu.VMEM((D_MODEL, tn), BF16), pltpu.VMEM((D_MODEL, tn), BF16),
                        pltpu.VMEM((8 + tm, tn), F32)],
        compiler_params=_cparams(("arbitrary", "arbitrary")),
        name="ffn_up",
    )(h2, h2, w_up, w_up, w_up, w_up, w_up, conv_w, conv_b.reshape(1, D_FF), sffn_t)


def _pad_lanes(v):
    return jnp.pad(v, ((0, 0), (0, LANES - v.shape[-1])))


def kernel(x_prompt, x_sample, c_prompt, c_sample, state_pool, state_conv, state_ssm, state_ffn, norm1, norm2, w_ada, b_ada, w_in, pool_w, pool_scale, conv_w, conv_b, dt_bias, a_log, d_skip, ssd_norm, w_out, w_up, ffn_conv_w, ffn_conv_b, w_down, norm_f):
    x_p = x_prompt.reshape(TP, D_MODEL)
    x_s = jnp.swapaxes(x_sample, 0, 1).reshape(TS, D_MODEL)
    c_all = jnp.concatenate([c_sample, c_prompt, jnp.zeros((4, D_MODEL), F32)], axis=0)
    mods, modp = _ada(c_all, w_ada, b_ada)
    modp = modp.reshape(DEPTH, BATCH, 1, 6 * D_MODEL)

    spool_t = jnp.swapaxes(state_pool, 1, 2)
    sconv_t = jnp.swapaxes(state_conv, 1, 2)
    sffn_t = jnp.swapaxes(state_ffn, 1, 2)

    w_in_t = jnp.swapaxes(w_in, 1, 2)

    d_skip_e = jnp.repeat(d_skip, SSD_HEAD_DIM, axis=-1)

    outs = {k: [] for k in ("pp", "pc", "ps", "pf", "sp", "sc", "sf")}
    h, x = _norm0(x_p, x_s, norm1[0], modp, mods, 0)
    nssm_stack = None
    for l in range(DEPTH):
        lw = {"pool_w": pool_w[l], "pool_scale": pool_scale[l].reshape(1, POOL_WIDTH),
              "conv_w": conv_w[l], "conv_b": conv_b[l].reshape(1, XBC_WIDTH),
              "dt_bias": _pad_lanes(dt_bias[l].reshape(1, SSD_HEADS)),
              "a_log": _pad_lanes(a_log[l].reshape(1, SSD_HEADS)),
              "d_skip_e": d_skip_e[l].reshape(1, SSD_WIDTH),
              "ssd_norm": ssd_norm[l].reshape(1, SSD_WIDTH)}
        proj, dtp = _inproj(h, w_in_t, l)
        mix_p, st_p, w_out_b = _mix_prompt(proj, dtp, lw, w_out, l)
        mix_s, nconv_s, nssm_stack = _mix_sample(proj, dtp, sconv_t, spool_t, state_ssm, lw, l, nssm_stack)
        x1, h2, w_down_b = _resid_proj(mix_p, mix_s.reshape(TS, D_MODEL), w_out_b, x, norm2[l], modp, mods, l, 2,
                                       (l, 3, 4), side_w=w_down)
        act_p, act_s, nffn_p, nffn_s = _ffn_up(h2, w_up, ffn_conv_w[l], ffn_conv_b[l], sffn_t, l)
        if l + 1 < DEPTH:
            x, h = _resid_proj(act_p, act_s, w_down_b, x1, norm1[l + 1], modp, mods, l, 5, (l + 1, 0, 1))
        else:
            y_p, y_s = _resid_proj(act_p, act_s, w_down_b, x1, norm_f, modp, mods, l, 5, None)

        ends = [(b + 1) * SEQ for b in range(BATCH)]
        outs["pp"].append(jnp.stack([lax.slice(proj, (e - POOL_BUF, 0), (e, POOL_WIDTH)) for e in ends]))
        outs["pc"].append(jnp.stack([lax.slice(proj, (e - (SSD_CONV - 1), POOL_WIDTH + SSD_WIDTH), (e, MAIN_WIDTH))
                                     for e in ends]))
        outs["ps"].append(jnp.transpose(st_p.reshape(BATCH, D_STATE, SSD_HEADS, SSD_HEAD_DIM), (0, 2, 3, 1)))
        outs["pf"].append(nffn_p)
        u_s = jnp.swapaxes(lax.slice(proj, (TP, 0), (T, POOL_WIDTH)).reshape(DEC_SEQ, DEC_BATCH, POOL_WIDTH), 0, 1)
        outs["sp"].append(jnp.concatenate([state_pool[l][:, DEC_SEQ:], u_s], axis=1))
        outs["sc"].append(jnp.swapaxes(nconv_s, 0, 1))
        outs["sf"].append(jnp.swapaxes(nffn_s, 0, 1))

    y_prompt = y_p.reshape(BATCH, SEQ, D_MODEL)
    y_sample = jnp.swapaxes(y_s.reshape(DEC_SEQ, DEC_BATCH, D_MODEL), 0, 1)
    return (y_prompt, y_sample,
            jnp.stack(outs["pp"]), jnp.stack(outs["pc"]), jnp.stack(outs["ps"]), jnp.stack(outs["pf"]),
            jnp.stack(outs["sp"]), jnp.stack(outs["sc"]), nssm_stack, jnp.stack(outs["sf"]))
```
